```python
import math
import jax, jax.numpy as jnp
from jax import lax
import numpy as np

D_MODEL = 1024
BATCH = 4
SEQ = 8192
DEPTH = 2

PLE_DIM = 256
N_EVEN = (DEPTH + 1) // 2
N_ODD = DEPTH // 2

DA_HEADS = 4
DA_HEAD_DIM = 64
DA_V_DIM = 2 * DA_HEAD_DIM
DA_QK_WIDTH = DA_HEADS * 2 * DA_HEAD_DIM
DA_WIDTH = DA_HEADS * DA_V_DIM

HG_HEADS = 4
HG_KDIM = 128
HG_VDIM = 128
HG_KWIDTH = HG_HEADS * HG_KDIM
HG_WIDTH = HG_HEADS * HG_VDIM
HG_CHUNK = 64

DL_HEADS = 16
DL_HEAD_DIM = D_MODEL // DL_HEADS
DL_PAIRS = ((128, 1), (512, 4), (2048, 16))

FFN_DIM = 4 * D_MODEL
ROPE_THETA = 500000.0
ROT_DIM = 64 // 4
ALPHA = (2 * DEPTH) ** 0.25
BETA = (8 * DEPTH) ** -0.25
LN_EPS = 1e-5
Q_BLOCK = 128

kernel_name = "hybrid_diffattn_hgrn2_dilated_deepnorm"


def layer_norm(x, g, b):
    xf = x.astype(jnp.float32)
    mu = jnp.mean(xf, -1, keepdims=True)
    var = jnp.mean(jnp.square(xf - mu), -1, keepdims=True)
    return ((xf - mu) * lax.rsqrt(var + LN_EPS) * g + b).astype(x.dtype)


def rms_norm(x, g):
    xf = x.astype(jnp.float32)
    return (xf * lax.rsqrt(jnp.mean(xf * xf, -1, keepdims=True) + LN_EPS) * g).astype(x.dtype)


def rope_tables(seq):
    inv = ROPE_THETA ** (-jnp.arange(0, ROT_DIM, 2, dtype=jnp.float32) / ROT_DIM)
    ang = jnp.arange(seq, dtype=jnp.float32)[:, None] * inv[None, :]
    return jnp.cos(ang), jnp.sin(ang)


def partial_rope(x, cos, sin):
    half = ROT_DIM // 2
    x1, x2, xp = x[..., :half], x[..., half:ROT_DIM], x[..., ROT_DIM:]
    c, s = cos.astype(x.dtype), sin.astype(x.dtype)
    return jnp.concatenate([x1 * c - x2 * s, x1 * s + x2 * c, xp], axis=-1)


def diff_attention(qa, ka, va, lam_params, sub_g, lam_init, cos, sin):
    B, S = qa.shape[:2]
    q = partial_rope(qa.transpose(0, 2, 3, 1, 4), cos, sin) * (DA_HEAD_DIM ** -0.5)
    k = partial_rope(ka.transpose(0, 2, 3, 1, 4), cos, sin)
    v = va.transpose(0, 2, 1, 3)
    lp = lam_params.astype(jnp.float32)
    lam = jnp.exp(jnp.sum(lp[0] * lp[1])) - jnp.exp(jnp.sum(lp[2] * lp[3])) + lam_init
    nb = S // Q_BLOCK
    qb = q.reshape(B, DA_HEADS, 2, nb, Q_BLOCK, DA_HEAD_DIM).transpose(3, 0, 1, 2, 4, 5)
    kpos = jnp.arange(S)

    def block(args):
        qi, i = args
        s = jnp.einsum('bhcqd,bhckd->bhcqk', qi, k).astype(jnp.float32)
        qpos = i * Q_BLOCK + jnp.arange(Q_BLOCK)
        s = jnp.where(kpos[None, :] <= qpos[:, None], s, -jnp.inf)
        pr = jax.nn.softmax(s, axis=-1)
        a = pr[:, :, 0] - lam * pr[:, :, 1]
        return jnp.einsum('bhqk,bhkd->bhqd', a.astype(v.dtype), v)

    o = lax.map(block, (qb, jnp.arange(nb)))
    o = o.transpose(1, 2, 0, 3, 4).reshape(B, DA_HEADS, S, DA_V_DIM)
    o = rms_norm(o, sub_g) * (1.0 - lam_init)
    return o.transpose(0, 2, 1, 3).reshape(B, S, DA_WIDTH)


def hgrn2(hq, hf, hi, hg, lb, out_g):
    B, S, H, dk = hq.shape
    dv = hi.shape[-1]
    f32 = jnp.float32
    q = jax.nn.silu(hq.astype(f32))
    f = lb + (1.0 - lb) * jax.nn.sigmoid(hf.astype(f32))
    kk = 1.0 - f
    logf = jnp.log(f)
    C = HG_CHUNK
    n = S // C

    def chunks(t):
        return t.reshape(B, n, C, H, t.shape[-1]).transpose(1, 0, 3, 2, 4)

    qc, kc, lc, vc = chunks(q), chunks(kk), chunks(logf), chunks(hi.astype(f32))
    b = jnp.cumsum(lc, axis=-2)
    b_last = b[..., C - 1:C, :]
    b_mid = b[..., C // 2 - 1:C // 2, :]
    a = jnp.einsum('nbhtk,nbhsk->nbhts', qc * jnp.exp(b - b_mid), kc * jnp.exp(b_mid - b))
    a = jnp.where(jnp.tril(jnp.ones((C, C), bool)), a, 0.0)
    o_intra = jnp.einsum('nbhts,nbhsv->nbhtv', a, vc)
    q_out = qc * jnp.exp(b)
    k_st = kc * jnp.exp(b_last - b)
    decay = jnp.exp(b_last[..., 0, :])

    def step(state, xs):
        qo, ks, vv, dc = xs
        o_inter = jnp.einsum('bhtk,bhkv->bhtv', qo, state)
        state = dc[..., None] * state + jnp.einsum('bhsk,bhsv->bhkv', ks, vv)
        return state, o_inter

    s0 = jnp.zeros((B, H, dk, dv), f32)
    _, o_inter = lax.scan(step, s0, (q_out, k_st, vc, decay))
    o = (o_intra + o_inter).transpose(1, 0, 3, 2, 4).reshape(B, S, H, dv)
    o = rms_norm(o, out_g) * jax.nn.silu(hg.astype(f32))
    return o.reshape(B, S, H * dv).astype(hi.dtype)


def even_mixer(x, w_in, w_out, lam_params, sub_g, lb, hg_g, lam_init, cos, sin):
    B, S, _ = x.shape
    h = x @ w_in
    sizes = (DA_QK_WIDTH, DA_QK_WIDTH, DA_WIDTH, HG_KWIDTH, HG_KWIDTH, HG_WIDTH, HG_WIDTH)
    cuts = [sum(sizes[:i + 1]) for i in range(len(sizes) - 1)]
    qa, ka, va, hq, hf, hi, hg = jnp.split(h, cuts, axis=-1)
    o_a = diff_attention(qa.reshape(B, S, DA_HEADS, 2, DA_HEAD_DIM),
                         ka.reshape(B, S, DA_HEADS, 2, DA_HEAD_DIM),
                         va.reshape(B, S, DA_HEADS, DA_V_DIM),
                         lam_params, sub_g, lam_init, cos, sin)
    o_b = hgrn2(hq.reshape(B, S, HG_HEADS, HG_KDIM), hf.reshape(B, S, HG_HEADS, HG_KDIM),
                hi.reshape(B, S, HG_HEADS, HG_VDIM), hg.reshape(B, S, HG_HEADS, HG_VDIM),
                lb.reshape(HG_HEADS, HG_KDIM), hg_g)
    return jnp.concatenate([o_a.astype(x.dtype), o_b.astype(x.dtype)], axis=-1) @ w_out


def dilated_branch(q, k, v, dil, span):
    B, H, S, hd = q.shape
    L = S // dil

    def to_res(t):
        return t.reshape(B, H, L, dil, hd).transpose(0, 1, 3, 2, 4)

    blk = span
    nb = -(-L // blk)
    pad = nb * blk - L

    def padl(t, front):
        return jnp.pad(t, ((0, 0), (0, 0), (0, 0), (front, pad), (0, 0)))

    qp = padl(to_res(q), 0).reshape(B, H, dil, nb, blk, hd)
    kp = padl(to_res(k), blk).reshape(B, H, dil, nb + 1, blk, hd)
    vp = padl(to_res(v), blk).reshape(B, H, dil, nb + 1, blk, hd)
    kw = jnp.concatenate([kp[:, :, :, :-1], kp[:, :, :, 1:]], axis=-2)
    vw = jnp.concatenate([vp[:, :, :, :-1], vp[:, :, :, 1:]], axis=-2)
    s = jnp.einsum('bhrnqd,bhrnkd->bhrnqk', qp, kw).astype(jnp.float32)
    a_idx = jnp.arange(blk)[:, None]
    c_idx = jnp.arange(2 * blk)[None, :]
    dist = blk + a_idx - c_idx
    key_idx = (jnp.arange(nb)[:, None, None] - 1) * blk + c_idx[None]
    valid = (dist >= 0)[None] & (dist <= span)[None] & (key_idx >= 0)
    s = jnp.where(valid, s, -jnp.inf)
    m = jnp.max(s, axis=-1, keepdims=True)
    e = jnp.exp(s - m)
    den = jnp.sum(e, axis=-1)
    o = jnp.einsum('bhrnqk,bhrnkd->bhrnqd', e.astype(v.dtype), vw) / den[..., None].astype(v.dtype)
    lse = m[..., 0] + jnp.log(den)
    o = o.reshape(B, H, dil, nb * blk, hd)[:, :, :, :L].transpose(0, 1, 3, 2, 4).reshape(B, H, S, hd)
    lse = lse.reshape(B, H, dil, nb * blk)[..., :L].transpose(0, 1, 3, 2).reshape(B, H, S)
    return o, lse


def odd_mixer(x, w_in, w_out, cos, sin):
    B, S, D = x.shape
    q, k, v = jnp.split(x @ w_in, 3, axis=-1)

    def heads(t):
        return t.reshape(B, S, DL_HEADS, DL_HEAD_DIM).transpose(0, 2, 1, 3)

    q = partial_rope(heads(q), cos, sin) * (DL_HEAD_DIM ** -0.5)
    k = partial_rope(heads(k), cos, sin)
    v = heads(v)
    outs, lses = [], []
    for (w, d) in DL_PAIRS:
        o, lse = dilated_branch(q, k, v, d, w // d)
        outs.append(o)
        lses.append(lse)
    wts = jax.nn.softmax(jnp.stack(lses, axis=0), axis=0)
    o = jnp.einsum('gbhs,gbhsd->bhsd', wts.astype(v.dtype), jnp.stack(outs, axis=0))
    return o.transpose(0, 2, 1, 3).reshape(B, S, D) @ w_out


def setup_inputs(seed: int = 0) -> dict:
    key = jax.random.key(seed)
    ks = jax.random.split(key, 20)
    f32 = jnp.float32

    def nrm(k, shape, s):
        return jax.random.normal(k, shape, f32) * s

    def gain(k, shape):
        return 1.0 + 0.02 * jax.random.normal(k, shape, f32)

    even_in = 2 * DA_QK_WIDTH + DA_WIDTH + 2 * HG_KWIDTH + 2 * HG_WIDTH
    return {
        "x": nrm(ks[0], (BATCH, SEQ, D_MODEL), 1.0),
        "p": nrm(ks[1], (DEPTH, BATCH, SEQ, PLE_DIM), 1.0),
        "ev_w_in": nrm(ks[2], (N_EVEN, D_MODEL, even_in), D_MODEL ** -0.5),
        "ev_w_out": nrm(ks[3], (N_EVEN, DA_WIDTH + HG_WIDTH, D_MODEL), BETA * (DA_WIDTH + HG_WIDTH) ** -0.5),
        "da_lambda": nrm(ks[4], (N_EVEN, 4, DA_HEAD_DIM), 0.1),
        "da_subln_g": gain(ks[5], (N_EVEN, DA_V_DIM)),
        "hg_lb_logits": nrm(ks[6], (N_EVEN + 1, HG_KWIDTH), 0.1),
        "hg_norm_g": gain(ks[7], (N_EVEN, HG_VDIM)),
        "od_w_in": nrm(ks[8], (N_ODD, D_MODEL, 3 * D_MODEL), D_MODEL ** -0.5),
        "od_w_out": nrm(ks[9], (N_ODD, D_MODEL, D_MODEL), BETA * D_MODEL ** -0.5),
        "ln1_g": gain(ks[10], (DEPTH, D_MODEL)),
        "ln1_b": nrm(ks[11], (DEPTH, D_MODEL), 0.02),
        "ffn_w1": nrm(ks[12], (DEPTH, D_MODEL, FFN_DIM), D_MODEL ** -0.5),
        "ffn_w2": nrm(ks[13], (DEPTH, FFN_DIM, D_MODEL), BETA * FFN_DIM ** -0.5),
        "ln2_g": gain(ks[14], (DEPTH, D_MODEL)),
        "ln2_b": nrm(ks[15], (DEPTH, D_MODEL), 0.02),
        "ple_w_proj": nrm(ks[16], (DEPTH, PLE_DIM, D_MODEL), PLE_DIM ** -0.5),
        "ple_w_gate": nrm(ks[17], (DEPTH, D_MODEL, D_MODEL), D_MODEL ** -0.5),
        "ple_norm_g": gain(ks[18], (DEPTH, D_MODEL)),
    }


def reference(x, p, ev_w_in, ev_w_out, da_lambda, da_subln_g, hg_lb_logits, hg_norm_g,
              od_w_in, od_w_out, ln1_g, ln1_b, ffn_w1, ffn_w2, ln2_g, ln2_b,
              ple_w_proj, ple_w_gate, ple_norm_g):
    S = x.shape[1]
    cos, sin = rope_tables(S)
    lb_all = jnp.cumsum(jax.nn.softmax(hg_lb_logits.astype(jnp.float32), axis=0), axis=0)
    for l in range(DEPTH):
        j = l // 2
        if l % 2 == 0:
            lam_init = 0.8 - 0.6 * math.exp(-0.3 * l)
            mix = even_mixer(x, ev_w_in[j], ev_w_out[j], da_lambda[j], da_subln_g[j],
                             lb_all[j], hg_norm_g[j], lam_init, cos, sin)
        else:
            mix = odd_mixer(x, od_w_in[j], od_w_out[j], cos, sin)
        x = layer_norm(ALPHA * x + mix, ln1_g[l], ln1_b[l])
        hdn = jnp.square(jax.nn.relu(x @ ffn_w1[l]))
        x = layer_norm(ALPHA * x + hdn @ ffn_w2[l], ln2_g[l], ln2_b[l])
        e = rms_norm(p[l] @ ple_w_proj[l], ple_norm_g[l])
        x = x + jax.nn.sigmoid(x @ ple_w_gate[l]) * e
    return x
```

```python
import functools
import math

import jax
import jax.numpy as jnp
from jax import lax
from jax.experimental import pallas as pl
from jax.experimental.pallas import tpu as pltpu

D_MODEL = 1024
DEPTH = 2
PLE_DIM = 256

DA_HEADS = 4
DA_HEAD_DIM = 64
DA_V_DIM = 2 * DA_HEAD_DIM
DA_QK_WIDTH = DA_HEADS * 2 * DA_HEAD_DIM
DA_WIDTH = DA_HEADS * DA_V_DIM

HG_HEADS = 4
HG_KDIM = 128
HG_VDIM = 128
HG_KWIDTH = HG_HEADS * HG_KDIM
HG_WIDTH = HG_HEADS * HG_VDIM
HG_CHUNK = 64

DL_HEADS = 16
DL_HEAD_DIM = D_MODEL // DL_HEADS
DL_PAIRS = ((128, 1), (512, 4), (2048, 16))
DL_SPAN = 128

FFN_DIM = 4 * D_MODEL
ROPE_THETA = 500000.0
ROT_DIM = 16
ALPHA = (2 * DEPTH) ** 0.25
LN_EPS = 1e-5

LANES = 128
VMEM_LIMIT_BYTES = 56 * 1024 * 1024

F32 = jnp.float32
BF16 = jnp.bfloat16
NEG_INF = float("-inf")


def _params(sem, vmem=VMEM_LIMIT_BYTES):
    return pltpu.CompilerParams(dimension_semantics=sem, vmem_limit_bytes=vmem)


def _resident(shape):
    nd = len(shape)
    return pl.BlockSpec(shape, lambda *_: (0,) * nd, pipeline_mode=pl.Buffered(1))


def _sigmoid(x):
    return 1.0 / (1.0 + jnp.exp(-x))


def _layer_norm(x, g, b):
    mu = jnp.mean(x, axis=-1, keepdims=True)
    xc = x - mu
    var = jnp.mean(xc * xc, axis=-1, keepdims=True)
    return xc * lax.rsqrt(var + LN_EPS) * g + b


def _rms_norm(x, g):
    return x * lax.rsqrt(jnp.mean(x * x, axis=-1, keepdims=True) + LN_EPS) * g


def _rope_lane_tables(seq):
    half = ROT_DIM // 2
    inv = ROPE_THETA ** (-jnp.arange(0, ROT_DIM, 2, dtype=F32) / ROT_DIM)
    ang = jnp.arange(seq, dtype=F32)[:, None] * inv[None, :]
    cos, sin = jnp.cos(ang), jnp.sin(ang)
    rest = 64 - ROT_DIM
    zh = jnp.zeros((seq, half), F32)
    c = jnp.concatenate([cos, cos, jnp.ones((seq, rest), F32)], axis=-1)
    s1 = jnp.concatenate([zh, sin, jnp.zeros((seq, rest), F32)], axis=-1)
    s2 = jnp.concatenate([-sin, zh, jnp.zeros((seq, rest), F32)], axis=-1)
    tile = lambda t: jnp.concatenate([t, t], axis=-1)
    return tile(c), tile(s1), tile(s2)


def _proj_kernel(x_ref, w_ref, c_ref, s1_ref, s2_ref, *out_refs, segs, col_chunk):
    xb = x_ref[...].astype(BF16)
    c, s1, s2 = c_ref[...], s1_ref[...], s2_ref[...]
    for out_ref, (start, width, kind) in zip(out_refs, segs):
        for j in range(0, width, col_chunk):
            acc = jnp.dot(xb, w_ref[:, start + j:start + j + col_chunk],
                          preferred_element_type=F32)
            if kind == "plain":
                out_ref[:, j:j + col_chunk] = acc.astype(out_ref.dtype)
                continue
            for g in range(col_chunk // LANES):
                a = acc[:, g * LANES:(g + 1) * LANES]
                r = (a * c + pltpu.roll(a, ROT_DIM // 2, 1) * s1
                     + pltpu.roll(a, LANES - ROT_DIM // 2, 1) * s2)
                if kind == "q":
                    r = r * (64 ** -0.5)
                out_ref[:, j + g * LANES:j + (g + 1) * LANES] = r.astype(out_ref.dtype)


def _project(x2, w_bf, tables, seq, segs, out_dtypes, tm=512, col_chunk=512):
    T, D = x2.shape
    N = w_bf.shape[1]
    n_seq_blocks = seq // tm
    tab_spec = pl.BlockSpec((tm, LANES), lambda i: (i % n_seq_blocks, 0))
    out_shape = [jax.ShapeDtypeStruct((T, w), dt) for (_, w, _), dt in zip(segs, out_dtypes)]
    out_specs = [pl.BlockSpec((tm, w), lambda i: (i, 0)) for (_, w, _) in segs]
    return pl.pallas_call(
        functools.partial(_proj_kernel, segs=segs, col_chunk=col_chunk),
        grid=(T // tm,),
        in_specs=[pl.BlockSpec((tm, D), lambda i: (i, 0)), _resident((D, N)),
                  tab_spec, tab_spec, tab_spec],
        out_specs=out_specs,
        out_shape=out_shape,
        compiler_params=_params(("parallel",)),
        name="in_proj",
    )(x2, w_bf, *tables)


def _diffattn_kernel(lam_ref, q_ref, k_ref, v_ref, g_ref, o_ref, m_sc, l_sc, acc_sc,
                     *, tq, lam_init):
    qi = pl.program_id(2)
    q = q_ref[0]
    lane = lax.broadcasted_iota(jnp.int32, (tq, LANES), 1)
    zero = jnp.zeros_like(q)
    qs = jnp.concatenate([jnp.where(lane < DA_HEAD_DIM, q, zero),
                          jnp.where(lane >= DA_HEAD_DIM, q, zero)], axis=0)
    m_sc[...] = jnp.full(m_sc.shape, NEG_INF, F32)
    l_sc[...] = jnp.zeros(l_sc.shape, F32)
    acc_sc[...] = jnp.zeros(acc_sc.shape, F32)

    def step(j, masked):
        start = pl.multiple_of(j * tq, tq)
        kj = k_ref[0, pl.ds(start, tq), :]
        vj = v_ref[0, pl.ds(start, tq), :]
        s = lax.dot_general(qs, kj, (((1,), (1,)), ((), ())), preferred_element_type=F32)
        if masked:
            row = lax.broadcasted_iota(jnp.int32, (2 * tq, tq), 0)
            col = lax.broadcasted_iota(jnp.int32, (2 * tq, tq), 1)
            row = jnp.where(row >= tq, row - tq, row)
            s = jnp.where(col <= row, s, NEG_INF)
        m_prev = m_sc[...]
        m_new = jnp.maximum(m_prev, jnp.max(s, axis=-1, keepdims=True))
        alpha = jnp.exp(m_prev - m_new)
        p = jnp.exp(s - m_new)
        l_sc[...] = alpha * l_sc[...] + jnp.sum(p, axis=-1, keepdims=True)
        acc_sc[...] = alpha * acc_sc[...] + jnp.dot(p.astype(BF16), vj,
                                                    preferred_element_type=F32)
        m_sc[...] = m_new

    def body(j, carry):
        step(j, False)
        return carry

    lax.fori_loop(0, qi, body, 0)
    step(qi, True)

    lam = lam_ref[0]
    o = (acc_sc[:tq, :] / l_sc[:tq, :]) - lam * (acc_sc[tq:, :] / l_sc[tq:, :])
    o = _rms_norm(o, g_ref[...]) * (1.0 - lam_init)
    o_ref[0] = o.astype(o_ref.dtype)


def _diff_attention(q, k, v, lam, sub_g, lam_init, tq=256):
    B, S, _ = q.shape
    return pl.pallas_call(
        functools.partial(_diffattn_kernel, tq=tq, lam_init=lam_init),
        grid=(B, DA_HEADS, S // tq),
        in_specs=[
            pl.BlockSpec(memory_space=pltpu.SMEM),
            pl.BlockSpec((1, tq, LANES), lambda b, h, i: (b, i, h)),
            pl.BlockSpec((1, S, LANES), lambda b, h, i: (b, 0, h)),
            pl.BlockSpec((1, S, LANES), lambda b, h, i: (b, 0, h)),
            pl.BlockSpec((1, DA_V_DIM), lambda b, h, i: (0, 0)),
        ],
        out_specs=pl.BlockSpec((1, tq, LANES), lambda b, h, i: (b, i, h)),
        out_shape=jax.ShapeDtypeStruct((B, S, DA_WIDTH), BF16),
        scratch_shapes=[pltpu.VMEM((2 * tq, 1), F32), pltpu.VMEM((2 * tq, 1), F32),
                        pltpu.VMEM((2 * tq, DA_V_DIM), F32)],
        compiler_params=_params(("parallel", "parallel", "arbitrary")),
        name="diff_attn",
    )(lam, q, k, v, sub_g)


def _hgrn_kernel(g_ref, lb_ref, og_ref, o_ref, state_sc, *, chunks):
    @pl.when(pl.program_id(1) == 0)
    def _():
        state_sc[...] = jnp.zeros(state_sc.shape, F32)

    C = HG_CHUNK
    ri = lax.broadcasted_iota(jnp.int32, (C, C), 0)
    ci = lax.broadcasted_iota(jnp.int32, (C, C), 1)
    tril = ci <= ri
    tri_f = tril.astype(F32)
    og = og_ref[...]
    for h in range(HG_HEADS):
        lb = lb_ref[:, h * HG_KDIM:(h + 1) * HG_KDIM]
        for c in range(chunks):
            rows = slice(c * C, (c + 1) * C)
            hq = g_ref[0, rows, h * HG_KDIM:(h + 1) * HG_KDIM]
            hf = g_ref[0, rows, HG_KWIDTH + h * HG_KDIM:HG_KWIDTH + (h + 1) * HG_KDIM]
            hi = g_ref[0, rows, 2 * HG_KWIDTH + h * HG_VDIM:2 * HG_KWIDTH + (h + 1) * HG_VDIM]
            hg = g_ref[0, rows, 2 * HG_KWIDTH + HG_WIDTH + h * HG_VDIM:
                       2 * HG_KWIDTH + HG_WIDTH + (h + 1) * HG_VDIM]
            q = hq * _sigmoid(hq)
            f = lb + (1.0 - lb) * _sigmoid(hf)
            kk = 1.0 - f
            logf = jnp.log(f)
            b = jnp.dot(tri_f, logf, preferred_element_type=F32,
                        precision=lax.Precision.HIGHEST)
            b_last = b[C - 1:C, :]
            b_mid = b[C // 2 - 1:C // 2, :]
            vb = hi.astype(BF16)
            a = lax.dot_general((q * jnp.exp(b - b_mid)).astype(BF16),
                                (kk * jnp.exp(b_mid - b)).astype(BF16),
                                (((1,), (1,)), ((), ())), preferred_element_type=F32)
            a = jnp.where(tril, a, 0.0)
            o = jnp.dot(a.astype(BF16), vb, preferred_element_type=F32)
            st = state_sc[h]
            o = o + lax.dot_general((q * jnp.exp(b)).astype(BF16), st.astype(BF16),
                                    (((1,), (1,)), ((), ())), preferred_element_type=F32)
            k_st = (kk * jnp.exp(b_last - b)).astype(BF16)
            upd = lax.dot_general(vb, k_st, (((0,), (0,)), ((), ())),
                                  preferred_element_type=F32)
            state_sc[h] = jnp.exp(b_last) * st + upd
            o = _rms_norm(o, og) * (hg * _sigmoid(hg))
            o_ref[0, rows, h * HG_VDIM:(h + 1) * HG_VDIM] = o.astype(o_ref.dtype)


def _hgrn2(g, lb, out_g, tc=256):
    B, S, W = g.shape
    return pl.pallas_call(
        functools.partial(_hgrn_kernel, chunks=tc // HG_CHUNK),
        grid=(B, S // tc),
        in_specs=[pl.BlockSpec((1, tc, W), lambda b, i: (b, i, 0)),
                  pl.BlockSpec((1, HG_KWIDTH), lambda b, i: (0, 0)),
                  pl.BlockSpec((1, HG_VDIM), lambda b, i: (0, 0))],
        out_specs=pl.BlockSpec((1, tc, HG_WIDTH), lambda b, i: (b, i, 0)),
        out_shape=jax.ShapeDtypeStruct((B, S, HG_WIDTH), BF16),
        scratch_shapes=[pltpu.VMEM((HG_HEADS, HG_VDIM, HG_KDIM), F32)],
        compiler_params=_params(("parallel", "arbitrary")),
        name="hgrn2",
    )(g, lb, out_g)


def _dil_kernel(*refs, tl, first, last):
    if first:
        q_ref, kp_ref, kc_ref, vp_ref, vc_ref = refs[:5]
        outs = refs[5:]
    else:
        q_ref, kp_ref, kc_ref, vp_ref, vc_ref, acc_in, ml_in = refs[:7]
        outs = refs[7:]
    i = pl.program_id(2)
    lane = lax.broadcasted_iota(jnp.int32, (tl, LANES), 1)
    low = lane < DL_HEAD_DIM
    row = lax.broadcasted_iota(jnp.int32, (2 * tl, 2 * tl), 0)
    col = lax.broadcasted_iota(jnp.int32, (2 * tl, 2 * tl), 1)
    row = jnp.where(row >= tl, row - tl, row)
    dist = tl + row - col
    valid = (dist >= 0) & (dist <= DL_SPAN) & ((col >= tl) | (i > 0))
    ml_new = jnp.zeros((tl, LANES), F32)
    for hp in range(DL_HEADS // 2):
        cols = slice(hp * LANES, (hp + 1) * LANES)
        qp = q_ref[0, :, cols]
        zero = jnp.zeros_like(qp)
        qs = jnp.concatenate([jnp.where(low, qp, zero), jnp.where(low, zero, qp)], axis=0)
        kk = jnp.concatenate([kp_ref[0, :, cols], kc_ref[0, :, cols]], axis=0)
        vv = jnp.concatenate([vp_ref[0, :, cols], vc_ref[0, :, cols]], axis=0)
        s = lax.dot_general(qs, kk, (((1,), (1,)), ((), ())), preferred_element_type=F32)
        s = jnp.where(valid, s, NEG_INF)
        m_blk = jnp.max(s, axis=-1, keepdims=True)
        if first:
            m_new = m_blk
        else:
            ml = ml_in[0]
            h0 = 2 * hp
            m_prev = jnp.concatenate([ml[:, h0:h0 + 1], ml[:, h0 + 1:h0 + 2]], axis=0)
            l_prev = jnp.concatenate([ml[:, DL_HEADS + h0:DL_HEADS + h0 + 1],
                                      ml[:, DL_HEADS + h0 + 1:DL_HEADS + h0 + 2]], axis=0)
            m_new = jnp.maximum(m_prev, m_blk)
            alpha = jnp.exp(m_prev - m_new)
        p = jnp.exp(s - m_new)
        l_new = jnp.sum(p, axis=-1, keepdims=True)
        pv = jnp.dot(p.astype(BF16), vv, preferred_element_type=F32)
        acc = jnp.where(low, pv[:tl], pv[tl:])
        if not first:
            l_new = l_new + alpha * l_prev
            acc = acc + jnp.where(low, alpha[:tl], alpha[tl:]) * acc_in[0, :, cols]
        if last:
            inv = jnp.where(low, 1.0 / l_new[:tl], 1.0 / l_new[tl:])
            outs[0][0, :, cols] = (acc * inv).astype(outs[0].dtype)
        else:
            outs[0][0, :, cols] = acc
            for t, hh in ((0, 2 * hp), (1, 2 * hp + 1)):
                rs = slice(t * tl, (t + 1) * tl)
                ml_new = jnp.where(lane == hh, m_new[rs], ml_new)
                ml_new = jnp.where(lane == DL_HEADS + hh, l_new[rs], ml_new)
    if not last:
        outs[1][0] = ml_new


def _dilated_branch(q, k, v, dil, state, last, tl=DL_SPAN):
    B, S, D = q.shape
    L = S // dil
    first = state is None
    view = lambda t: t.reshape(B, L, dil * t.shape[-1])
    cur = lambda b, r, i: (b, i, r)
    prev = lambda b, r, i: (b, jnp.maximum(i - 1, 0), r)
    wide = lambda im: pl.BlockSpec((1, tl, D), im)
    in_specs = [wide(cur), wide(prev), wide(cur), wide(prev), wide(cur)]
    args = [view(q), view(k), view(k), view(v), view(v)]
    if not first:
        in_specs += [wide(cur), pl.BlockSpec((1, tl, LANES), cur)]
        args += [view(state[0]), view(state[1])]
    if last:
        out_shape = [jax.ShapeDtypeStruct((B, L, dil * D), BF16)]
        out_specs = [wide(cur)]
    else:
        out_shape = [jax.ShapeDtypeStruct((B, L, dil * D), F32),
                     jax.ShapeDtypeStruct((B, L, dil * LANES), F32)]
        out_specs = [wide(cur), pl.BlockSpec((1, tl, LANES), cur)]
    outs = pl.pallas_call(
        functools.partial(_dil_kernel, tl=tl, first=first, last=last),
        grid=(B, dil, L // tl),
        in_specs=in_specs,
        out_specs=out_specs,
        out_shape=out_shape,
        compiler_params=_params(("parallel", "parallel", "arbitrary")),
        name=f"dilated_attn_d{dil}",
    )(*args)
    if last:
        return outs[0].reshape(B, S, D)
    return outs[0].reshape(B, S, D), outs[1].reshape(B, S, LANES)


def _dilated_attention(q, k, v):
    order = sorted(DL_PAIRS, key=lambda wd: -wd[1])
    assert all(w // d == DL_SPAN for w, d in order)
    state = None
    for n, (_, d) in enumerate(order):
        state = _dilated_branch(q, k, v, d, state, last=(n == len(order) - 1))
    return state


def _tail_kernel(*refs, n_mix, ffn_chunk):
    x_ref = refs[0]
    mix_refs = refs[1:1 + n_mix]
    wo_refs = refs[1 + n_mix:1 + 2 * n_mix]
    (g1_ref, b1_ref, w1_ref, w2_ref, g2_ref, b2_ref,
     p_ref, wp_ref, wg_ref, pg_ref, o_ref) = refs[1 + 2 * n_mix:]
    mix = jnp.dot(mix_refs[0][...], wo_refs[0][...], preferred_element_type=F32)
    for m_ref, w_ref in zip(mix_refs[1:], wo_refs[1:]):
        mix = mix + jnp.dot(m_ref[...], w_ref[...], preferred_element_type=F32)
    x1 = _layer_norm(ALPHA * x_ref[...] + mix, g1_ref[...], b1_ref[...])
    x1b = x1.astype(BF16)
    ffn = None
    for c in range(0, FFN_DIM, ffn_chunk):
        hdn = jnp.dot(x1b, w1_ref[:, c:c + ffn_chunk], preferred_element_type=F32)
        hdn = jnp.square(jnp.maximum(hdn, 0.0)).astype(BF16)
        part = jnp.dot(hdn, w2_ref[c:c + ffn_chunk, :], preferred_element_type=F32)
        ffn = part if ffn is None else ffn + part
    x2 = _layer_norm(ALPHA * x1 + ffn, g2_ref[...], b2_ref[...])
    e = _rms_norm(jnp.dot(p_ref[...].astype(BF16), wp_ref[...], preferred_element_type=F32),
                  pg_ref[...])
    gate = _sigmoid(jnp.dot(x2.astype(BF16), wg_ref[...], preferred_element_type=F32))
    o_ref[...] = x2 + gate * e


def _layer_tail(x2, mixes, w_outs, g1, b1, w1, w2, g2, b2, p2, wp, wg, pg, tm=256,
                ffn_chunk=1024):
    T, D = x2.shape
    rows = lambda w: pl.BlockSpec((tm, w), lambda i: (i, 0))
    vec = lambda a: a.reshape(1, -1)
    in_specs = ([rows(D)] + [rows(m.shape[1]) for m in mixes]
                + [_resident(w.shape) for w in w_outs]
                + [_resident((1, D)), _resident((1, D)), _resident(w1.shape),
                   _resident(w2.shape), _resident((1, D)), _resident((1, D)),
                   rows(PLE_DIM), _resident(wp.shape), _resident(wg.shape),
                   _resident((1, D))])
    return pl.pallas_call(
        functools.partial(_tail_kernel, n_mix=len(mixes), ffn_chunk=ffn_chunk),
        grid=(T // tm,),
        in_specs=in_specs,
        out_specs=rows(D),
        out_shape=jax.ShapeDtypeStruct((T, D), F32),
        compiler_params=_params(("parallel",)),
        name="layer_tail",
    )(x2, *mixes, *w_outs, vec(g1), vec(b1), w1, w2, vec(g2), vec(b2), p2, wp, wg, vec(pg))


def kernel(x, p, ev_w_in, ev_w_out, da_lambda, da_subln_g, hg_lb_logits, hg_norm_g,
           od_w_in, od_w_out, ln1_g, ln1_b, ffn_w1, ffn_w2, ln2_g, ln2_b,
           ple_w_proj, ple_w_gate, ple_norm_g):
    B, S, D = x.shape
    T = B * S
    tables = _rope_lane_tables(S)
    lb_all = jnp.cumsum(jax.nn.softmax(hg_lb_logits.astype(F32), axis=0), axis=0)
    bf = lambda w: w.astype(BF16)
    x2 = x.reshape(T, D)
    for l in range(DEPTH):
        j = l // 2
        if l % 2 == 0:
            lam_init = 0.8 - 0.6 * math.exp(-0.3 * l)
            lp = da_lambda[j].astype(F32)
            lam = (jnp.exp(jnp.sum(lp[0] * lp[1])) - jnp.exp(jnp.sum(lp[2] * lp[3]))
                   + lam_init).reshape(1)
            segs = (("q", DA_QK_WIDTH), ("k", DA_QK_WIDTH), ("plain", DA_WIDTH),
                    ("plain", 2 * HG_KWIDTH + 2 * HG_WIDTH))
            starts = [sum(w for _, w in segs[:n]) for n in range(len(segs))]
            segs = tuple((s, w, kind) for s, (kind, w) in zip(starts, segs))
            q, k, v, g = _project(x2, bf(ev_w_in[j]), tables, S, segs, (BF16, BF16, BF16, F32))
            o_a = _diff_attention(q.reshape(B, S, -1), k.reshape(B, S, -1), v.reshape(B, S, -1),
                                  lam, da_subln_g[j].reshape(1, -1), lam_init)
            o_b = _hgrn2(g.reshape(B, S, -1), lb_all[j].reshape(1, -1),
                         hg_norm_g[j].reshape(1, -1))
            mixes = [o_a.reshape(T, -1), o_b.reshape(T, -1)]
            w_o = bf(ev_w_out[j])
            w_outs = [w_o[:DA_WIDTH], w_o[DA_WIDTH:]]
        else:
            segs = ((0, D, "q"), (D, D, "k"), (2 * D, D, "plain"))
            q, k, v = _project(x2, bf(od_w_in[j]), tables, S, segs, (BF16, BF16, BF16))
            o = _dilated_attention(q.reshape(B, S, D), k.reshape(B, S, D), v.reshape(B, S, D))
            mixes = [o.reshape(T, D)]
            w_outs = [bf(od_w_out[j])]
        x2 = _layer_tail(x2, mixes, w_outs, ln1_g[l], ln1_b[l], bf(ffn_w1[l]), bf(ffn_w2[l]),
                         ln2_g[l], ln2_b[l], p[l].reshape(T, PLE_DIM), bf(ple_w_proj[l]),
                         bf(ple_w_gate[l]), ple_norm_g[l])
    return x2.reshape(B, S, D)
```

```python
import functools
import math

import jax
import jax.numpy as jnp
from jax import lax
from jax.experimental import pallas as pl
from jax.experimental.pallas import tpu as pltpu

D_MODEL = 1024
DEPTH = 2
PLE_DIM = 256

DA_HEADS = 4
DA_HEAD_DIM = 64
DA_V_DIM = 2 * DA_HEAD_DIM
DA_QK_WIDTH = DA_HEADS * 2 * DA_HEAD_DIM
DA_WIDTH = DA_HEADS * DA_V_DIM

HG_HEADS = 4
HG_KDIM = 128
HG_VDIM = 128
HG_KWIDTH = HG_HEADS * HG_KDIM
HG_WIDTH = HG_HEADS * HG_VDIM
HG_CHUNK = 64

DL_HEADS = 16
DL_HEAD_DIM = D_MODEL // DL_HEADS
DL_PAIRS = ((128, 1), (512, 4), (2048, 16))
DL_SPAN = 128

FFN_DIM = 4 * D_MODEL
ROPE_THETA = 500000.0
ROT_DIM = 16
ALPHA = (2 * DEPTH) ** 0.25
LN_EPS = 1e-5

LANES = 128
VMEM_LIMIT_BYTES = 56 * 1024 * 1024

F32 = jnp.float32
BF16 = jnp.bfloat16
NEG_INF = float("-inf")


def _params(sem, vmem=VMEM_LIMIT_BYTES):
    return pltpu.CompilerParams(dimension_semantics=sem, vmem_limit_bytes=vmem)


def _resident(shape):
    nd = len(shape)
    return pl.BlockSpec(shape, lambda *_: (0,) * nd, pipeline_mode=pl.Buffered(1))


def _sigmoid(x):
    return 1.0 / (1.0 + jnp.exp(-x))


def _layer_norm(x, g, b):
    mu = jnp.mean(x, axis=-1, keepdims=True)
    xc = x - mu
    var = jnp.mean(xc * xc, axis=-1, keepdims=True)
    return xc * lax.rsqrt(var + LN_EPS) * g + b


def _rms_norm(x, g):
    return x * lax.rsqrt(jnp.mean(x * x, axis=-1, keepdims=True) + LN_EPS) * g


def _rope_lane_tables(seq):
    half = ROT_DIM // 2
    inv = ROPE_THETA ** (-jnp.arange(0, ROT_DIM, 2, dtype=F32) / ROT_DIM)
    ang = jnp.arange(seq, dtype=F32)[:, None] * inv[None, :]
    cos, sin = jnp.cos(ang), jnp.sin(ang)
    rest = 64 - ROT_DIM
    zh = jnp.zeros((seq, half), F32)
    c = jnp.concatenate([cos, cos, jnp.ones((seq, rest), F32)], axis=-1)
    s1 = jnp.concatenate([zh, sin, jnp.zeros((seq, rest), F32)], axis=-1)
    s2 = jnp.concatenate([-sin, zh, jnp.zeros((seq, rest), F32)], axis=-1)
    tile = lambda t: jnp.concatenate([t, t], axis=-1)
    return tile(c), tile(s1), tile(s2)


def _proj_kernel(x_ref, w_ref, c_ref, s1_ref, s2_ref, *out_refs, segs, col_chunk):
    xb = x_ref[...].astype(BF16)
    c, s1, s2 = c_ref[...], s1_ref[...], s2_ref[...]
    for out_ref, (start, width, kind) in zip(out_refs, segs):
        for j in range(0, width, col_chunk):
            acc = jnp.dot(xb, w_ref[:, start + j:start + j + col_chunk],
                          preferred_element_type=F32)
            if kind == "plain":
                out_ref[:, j:j + col_chunk] = acc.astype(out_ref.dtype)
                continue
            for g in range(col_chunk // LANES):
                a = acc[:, g * LANES:(g + 1) * LANES]
                r = (a * c + pltpu.roll(a, ROT_DIM // 2, 1) * s1
                     + pltpu.roll(a, LANES - ROT_DIM // 2, 1) * s2)
                if kind == "q":
                    r = r * (64 ** -0.5)
                out_ref[:, j + g * LANES:j + (g + 1) * LANES] = r.astype(out_ref.dtype)


def _project(x2, w_bf, tables, seq, segs, out_dtypes, tm=512, col_chunk=512):
    T, D = x2.shape
    N = w_bf.shape[1]
    n_seq_blocks = seq // tm
    tab_spec = pl.BlockSpec((tm, LANES), lambda i: (i % n_seq_blocks, 0))
    out_shape = [jax.ShapeDtypeStruct((T, w), dt) for (_, w, _), dt in zip(segs, out_dtypes)]
    out_specs = [pl.BlockSpec((tm, w), lambda i: (i, 0)) for (_, w, _) in segs]
    return pl.pallas_call(
        functools.partial(_proj_kernel, segs=segs, col_chunk=col_chunk),
        grid=(T // tm,),
        in_specs=[pl.BlockSpec((tm, D), lambda i: (i, 0)), _resident((D, N)),
                  tab_spec, tab_spec, tab_spec],
        out_specs=out_specs,
        out_shape=out_shape,
        compiler_params=_params(("parallel",)),
        name="in_proj",
    )(x2, w_bf, *tables)


def _diffattn_kernel(lam_ref, q_ref, k_ref, v_ref, g_ref, o_ref, acc_sc,
                     *, tq, unroll, lam_init):
    qi = pl.program_id(2)
    q = q_ref[0]
    lane = lax.broadcasted_iota(jnp.int32, (tq, LANES), 1)
    zero = jnp.zeros_like(q)
    qs = jnp.concatenate([jnp.where(lane < DA_HEAD_DIM, q, zero),
                          jnp.where(lane >= DA_HEAD_DIM, q, zero)], axis=0)
    acc_sc[...] = jnp.zeros(acc_sc.shape, F32)

    def scores(blk):
        start = pl.multiple_of(blk * tq, tq)
        return lax.dot_general(k_ref[0, pl.ds(start, tq), :], qs, (((1,), (1,)), ((), ())),
                               preferred_element_type=F32)

    def update(s, blk, carry, masked):
        m_prev, l_prev = carry
        if masked:
            krow = lax.broadcasted_iota(jnp.int32, (tq, 2 * tq), 0)
            col = lax.broadcasted_iota(jnp.int32, (tq, 2 * tq), 1)
            s = jnp.where(krow <= jnp.where(col >= tq, col - tq, col), s, NEG_INF)
        m_new = jnp.maximum(m_prev, jnp.max(s, axis=0, keepdims=True))
        alpha = jnp.exp(m_prev - m_new)
        p = jnp.exp(s - m_new)
        l_new = alpha * l_prev + jnp.sum(p, axis=0, keepdims=True)
        vj = v_ref[0, pl.ds(pl.multiple_of(blk * tq, tq), tq), :]
        pv = lax.dot_general(vj, p.astype(BF16), (((0,), (0,)), ((), ())),
                             preferred_element_type=F32)
        acc_sc[...] = alpha * acc_sc[...] + pv
        return m_new, l_new

    def run(first_blk, n, carry, diag_last):
        s_next = scores(first_blk)
        for u in range(n):
            s_cur = s_next
            if u + 1 < n:
                s_next = scores(first_blk + u + 1)
            carry = update(s_cur, first_blk + u, carry, diag_last and u == n - 1)
        return carry

    n_groups = qi // unroll
    carry = (jnp.full((1, 2 * tq), NEG_INF, F32), jnp.zeros((1, 2 * tq), F32))
    carry = lax.fori_loop(0, n_groups, lambda j, c: run(j * unroll, unroll, c, False), carry)
    tails = [functools.partial(run, n_groups * unroll, r + 1, diag_last=True)
             for r in range(unroll)]
    m, l = lax.switch(qi - n_groups * unroll, tails, carry)

    acc = acc_sc[...] / l
    o = acc[:, :tq] - lam_ref[0] * acc[:, tq:]
    o = o * lax.rsqrt(jnp.mean(o * o, axis=0, keepdims=True) + LN_EPS)
    o_ref[0] = (o.T * g_ref[...] * (1.0 - lam_init)).astype(o_ref.dtype)


def _diff_attention(q, k, v, lam, sub_g, lam_init, tq=256, unroll=4):
    B, S, _ = q.shape
    return pl.pallas_call(
        functools.partial(_diffattn_kernel, tq=tq, unroll=unroll, lam_init=lam_init),
        grid=(B, DA_HEADS, S // tq),
        in_specs=[
            pl.BlockSpec(memory_space=pltpu.SMEM),
            pl.BlockSpec((1, tq, LANES), lambda b, h, i: (b, i, h)),
            pl.BlockSpec((1, S, LANES), lambda b, h, i: (b, 0, h)),
            pl.BlockSpec((1, S, LANES), lambda b, h, i: (b, 0, h)),
            pl.BlockSpec((1, DA_V_DIM), lambda b, h, i: (0, 0)),
        ],
        out_specs=pl.BlockSpec((1, tq, LANES), lambda b, h, i: (b, i, h)),
        out_shape=jax.ShapeDtypeStruct((B, S, DA_WIDTH), BF16),
        scratch_shapes=[pltpu.VMEM((DA_V_DIM, 2 * tq), F32)],
        compiler_params=_params(("parallel", "parallel", "arbitrary")),
        name="diff_attn",
    )(lam, q, k, v, sub_g)


def _hgrn_kernel(g_ref, lb_ref, og_ref, o_ref, state_sc, *, chunks):
    @pl.when(pl.program_id(1) == 0)
    def _():
        state_sc[...] = jnp.zeros(state_sc.shape, F32)

    C = HG_CHUNK
    ri = lax.broadcasted_iota(jnp.int32, (C, C), 0)
    ci = lax.broadcasted_iota(jnp.int32, (C, C), 1)
    tril = ci <= ri
    tri_f = tril.astype(F32)
    og = og_ref[...]
    for h in range(HG_HEADS):
        lb = lb_ref[:, h * HG_KDIM:(h + 1) * HG_KDIM]
        for c in range(chunks):
            rows = slice(c * C, (c + 1) * C)
            hq = g_ref[0, rows, h * HG_KDIM:(h + 1) * HG_KDIM]
            hf = g_ref[0, rows, HG_KWIDTH + h * HG_KDIM:HG_KWIDTH + (h + 1) * HG_KDIM]
            hi = g_ref[0, rows, 2 * HG_KWIDTH + h * HG_VDIM:2 * HG_KWIDTH + (h + 1) * HG_VDIM]
            hg = g_ref[0, rows, 2 * HG_KWIDTH + HG_WIDTH + h * HG_VDIM:
                       2 * HG_KWIDTH + HG_WIDTH + (h + 1) * HG_VDIM]
            q = hq * _sigmoid(hq)
            f = lb + (1.0 - lb) * _sigmoid(hf)
            kk = 1.0 - f
            logf = jnp.log(f)
            b = jnp.dot(tri_f, logf, preferred_element_type=F32,
                        precision=lax.Precision.HIGHEST)
            b_last = b[C - 1:C, :]
            b_mid = b[C // 2 - 1:C // 2, :]
            vb = hi.astype(BF16)
            a = lax.dot_general((q * jnp.exp(b - b_mid)).astype(BF16),
                                (kk * jnp.exp(b_mid - b)).astype(BF16),
                                (((1,), (1,)), ((), ())), preferred_element_type=F32)
            a = jnp.where(tril, a, 0.0)
            o = jnp.dot(a.astype(BF16), vb, preferred_element_type=F32)
            st = state_sc[h]
            o = o + lax.dot_general((q * jnp.exp(b)).astype(BF16), st.astype(BF16),
                                    (((1,), (1,)), ((), ())), preferred_element_type=F32)
            k_st = (kk * jnp.exp(b_last - b)).astype(BF16)
            upd = lax.dot_general(vb, k_st, (((0,), (0,)), ((), ())),
                                  preferred_element_type=F32)
            state_sc[h] = jnp.exp(b_last) * st + upd
            o = _rms_norm(o, og) * (hg * _sigmoid(hg))
            o_ref[0, rows, h * HG_VDIM:(h + 1) * HG_VDIM] = o.astype(o_ref.dtype)


def _hgrn2(g, lb, out_g, tc=256):
    B, S, W = g.shape
    return pl.pallas_call(
        functools.partial(_hgrn_kernel, chunks=tc // HG_CHUNK),
        grid=(B, S // tc),
        in_specs=[pl.BlockSpec((1, tc, W), lambda b, i: (b, i, 0)),
                  pl.BlockSpec((1, HG_KWIDTH), lambda b, i: (0, 0)),
                  pl.BlockSpec((1, HG_VDIM), lambda b, i: (0, 0))],
        out_specs=pl.BlockSpec((1, tc, HG_WIDTH), lambda b, i: (b, i, 0)),
        out_shape=jax.ShapeDtypeStruct((B, S, HG_WIDTH), BF16),
        scratch_shapes=[pltpu.VMEM((HG_HEADS, HG_VDIM, HG_KDIM), F32)],
        compiler_params=_params(("parallel", "arbitrary")),
        name="hgrn2",
    )(g, lb, out_g)


def _dil_kernel(*refs, tl, first, last):
    if first:
        q_ref, kp_ref, kc_ref, vp_ref, vc_ref = refs[:5]
        outs = refs[5:]
    else:
        q_ref, kp_ref, kc_ref, vp_ref, vc_ref, acc_in, ml_in = refs[:7]
        outs = refs[7:]
    i = pl.program_id(2)
    lane = lax.broadcasted_iota(jnp.int32, (tl, LANES), 1)
    low = lane < DL_HEAD_DIM
    row = lax.broadcasted_iota(jnp.int32, (2 * tl, 2 * tl), 0)
    col = lax.broadcasted_iota(jnp.int32, (2 * tl, 2 * tl), 1)
    row = jnp.where(row >= tl, row - tl, row)
    dist = tl + row - col
    valid = (dist >= 0) & (dist <= DL_SPAN) & ((col >= tl) | (i > 0))
    ml_new = jnp.zeros((tl, LANES), F32)
    for hp in range(DL_HEADS // 2):
        cols = slice(hp * LANES, (hp + 1) * LANES)
        qp = q_ref[0, :, cols]
        zero = jnp.zeros_like(qp)
        qs = jnp.concatenate([jnp.where(low, qp, zero), jnp.where(low, zero, qp)], axis=0)
        kk = jnp.concatenate([kp_ref[0, :, cols], kc_ref[0, :, cols]], axis=0)
        vv = jnp.concatenate([vp_ref[0, :, cols], vc_ref[0, :, cols]], axis=0)
        s = lax.dot_general(qs, kk, (((1,), (1,)), ((), ())), preferred_element_type=F32)
        s = jnp.where(valid, s, NEG_INF)
        m_blk = jnp.max(s, axis=-1, keepdims=True)
        if first:
            m_new = m_blk
        else:
            ml = ml_in[0]
            h0 = 2 * hp
            m_prev = jnp.concatenate([ml[:, h0:h0 + 1], ml[:, h0 + 1:h0 + 2]], axis=0)
            l_prev = jnp.concatenate([ml[:, DL_HEADS + h0:DL_HEADS + h0 + 1],
                                      ml[:, DL_HEADS + h0 + 1:DL_HEADS + h0 + 2]], axis=0)
            m_new = jnp.maximum(m_prev, m_blk)
            alpha = jnp.exp(m_prev - m_new)
        p = jnp.exp(s - m_new)
        l_new = jnp.sum(p, axis=-1, keepdims=True)
        pv = jnp.dot(p.astype(BF16), vv, preferred_element_type=F32)
        acc = jnp.where(low, pv[:tl], pv[tl:])
        if not first:
            l_new = l_new + alpha * l_prev
            acc = acc + jnp.where(low, alpha[:tl], alpha[tl:]) * acc_in[0, :, cols]
        if last:
            inv = jnp.where(low, 1.0 / l_new[:tl], 1.0 / l_new[tl:])
            outs[0][0, :, cols] = (acc * inv).astype(outs[0].dtype)
        else:
            outs[0][0, :, cols] = acc
            for t, hh in ((0, 2 * hp), (1, 2 * hp + 1)):
                rs = slice(t * tl, (t + 1) * tl)
                ml_new = jnp.where(lane == hh, m_new[rs], ml_new)
                ml_new = jnp.where(lane == DL_HEADS + hh, l_new[rs], ml_new)
    if not last:
        outs[1][0] = ml_new


def _dilated_branch(q, k, v, dil, state, last, tl=DL_SPAN):
    B, S, D = q.shape
    L = S // dil
    first = state is None
    view = lambda t: t.reshape(B, L, dil * t.shape[-1])
    cur = lambda b, r, i: (b, i, r)
    prev = lambda b, r, i: (b, jnp.maximum(i - 1, 0), r)
    wide = lambda im: pl.BlockSpec((1, tl, D), im)
    in_specs = [wide(cur), wide(prev), wide(cur), wide(prev), wide(cur)]
    args = [view(q), view(k), view(k), view(v), view(v)]
    if not first:
        in_specs += [wide(cur), pl.BlockSpec((1, tl, LANES), cur)]
        args += [view(state[0]), view(state[1])]
    if last:
        out_shape = [jax.ShapeDtypeStruct((B, L, dil * D), BF16)]
        out_specs = [wide(cur)]
    else:
        out_shape = [jax.ShapeDtypeStruct((B, L, dil * D), F32),
                     jax.ShapeDtypeStruct((B, L, dil * LANES), F32)]
        out_specs = [wide(cur), pl.BlockSpec((1, tl, LANES), cur)]
    outs = pl.pallas_call(
        functools.partial(_dil_kernel, tl=tl, first=first, last=last),
        grid=(B, dil, L // tl),
        in_specs=in_specs,
        out_specs=out_specs,
        out_shape=out_shape,
        compiler_params=_params(("parallel", "parallel", "arbitrary")),
        name=f"dilated_attn_d{dil}",
    )(*args)
    if last:
        return outs[0].reshape(B, S, D)
    return outs[0].reshape(B, S, D), outs[1].reshape(B, S, LANES)


def _dilated_attention(q, k, v):
    order = sorted(DL_PAIRS, key=lambda wd: -wd[1])
    assert all(w // d == DL_SPAN for w, d in order)
    state = None
    for n, (_, d) in enumerate(order):
        state = _dilated_branch(q, k, v, d, state, last=(n == len(order) - 1))
    return state


def _tail_kernel(*refs, n_mix, ffn_chunk):
    x_ref = refs[0]
    mix_refs = refs[1:1 + n_mix]
    wo_refs = refs[1 + n_mix:1 + 2 * n_mix]
    (g1_ref, b1_ref, w1_ref, w2_ref, g2_ref, b2_ref,
     p_ref, wp_ref, wg_ref, pg_ref, o_ref) = refs[1 + 2 * n_mix:]
    mix = jnp.dot(mix_refs[0][...], wo_refs[0][...], preferred_element_type=F32)
    for m_ref, w_ref in zip(mix_refs[1:], wo_refs[1:]):
        mix = mix + jnp.dot(m_ref[...], w_ref[...], preferred_element_type=F32)
    x1 = _layer_norm(ALPHA * x_ref[...] + mix, g1_ref[...], b1_ref[...])
    x1b = x1.astype(BF16)
    ffn = None
    for c in range(0, FFN_DIM, ffn_chunk):
        hdn = jnp.dot(x1b, w1_ref[:, c:c + ffn_chunk], preferred_element_type=F32)
        hdn = jnp.square(jnp.maximum(hdn, 0.0)).astype(BF16)
        part = jnp.dot(hdn, w2_ref[c:c + ffn_chunk, :], preferred_element_type=F32)
        ffn = part if ffn is None else ffn + part
    x2 = _layer_norm(ALPHA * x1 + ffn, g2_ref[...], b2_ref[...])
    e = _rms_norm(jnp.dot(p_ref[...].astype(BF16), wp_ref[...], preferred_element_type=F32),
                  pg_ref[...])
    gate = _sigmoid(jnp.dot(x2.astype(BF16), wg_ref[...], preferred_element_type=F32))
    o_ref[...] = x2 + gate * e


def _layer_tail(x2, mixes, w_outs, g1, b1, w1, w2, g2, b2, p2, wp, wg, pg, tm=256,
                ffn_chunk=1024):
    T, D = x2.shape
    rows = lambda w: pl.BlockSpec((tm, w), lambda i: (i, 0))
    vec = lambda a: a.reshape(1, -1)
    in_specs = ([rows(D)] + [rows(m.shape[1]) for m in mixes]
                + [_resident(w.shape) for w in w_outs]
                + [_resident((1, D)), _resident((1, D)), _resident(w1.shape),
                   _resident(w2.shape), _resident((1, D)), _resident((1, D)),
                   rows(PLE_DIM), _resident(wp.shape), _resident(wg.shape),
                   _resident((1, D))])
    return pl.pallas_call(
        functools.partial(_tail_kernel, n_mix=len(mixes), ffn_chunk=ffn_chunk),
        grid=(T // tm,),
        in_specs=in_specs,
        out_specs=rows(D),
        out_shape=jax.ShapeDtypeStruct((T, D), F32),
        compiler_params=_params(("parallel",)),
        name="layer_tail",
    )(x2, *mixes, *w_outs, vec(g1), vec(b1), w1, w2, vec(g2), vec(b2), p2, wp, wg, vec(pg))


def kernel(x, p, ev_w_in, ev_w_out, da_lambda, da_subln_g, hg_lb_logits, hg_norm_g,
           od_w_in, od_w_out, ln1_g, ln1_b, ffn_w1, ffn_w2, ln2_g, ln2_b,
           ple_w_proj, ple_w_gate, ple_norm_g):
    B, S, D = x.shape
    T = B * S
    tables = _rope_lane_tables(S)
    lb_all = jnp.cumsum(jax.nn.softmax(hg_lb_logits.astype(F32), axis=0), axis=0)
    bf = lambda w: w.astype(BF16)
    x2 = x.reshape(T, D)
    for l in range(DEPTH):
        j = l // 2
        if l % 2 == 0:
            lam_init = 0.8 - 0.6 * math.exp(-0.3 * l)
            lp = da_lambda[j].astype(F32)
            lam = (jnp.exp(jnp.sum(lp[0] * lp[1])) - jnp.exp(jnp.sum(lp[2] * lp[3]))
                   + lam_init).reshape(1)
            segs = (("q", DA_QK_WIDTH), ("k", DA_QK_WIDTH), ("plain", DA_WIDTH),
                    ("plain", 2 * HG_KWIDTH + 2 * HG_WIDTH))
            starts = [sum(w for _, w in segs[:n]) for n in range(len(segs))]
            segs = tuple((s, w, kind) for s, (kind, w) in zip(starts, segs))
            q, k, v, g = _project(x2, bf(ev_w_in[j]), tables, S, segs, (BF16, BF16, BF16, F32))
            o_a = _diff_attention(q.reshape(B, S, -1), k.reshape(B, S, -1), v.reshape(B, S, -1),
                                  lam, da_subln_g[j].reshape(1, -1), lam_init)
            o_b = _hgrn2(g.reshape(B, S, -1), lb_all[j].reshape(1, -1),
                         hg_norm_g[j].reshape(1, -1))
            mixes = [o_a.reshape(T, -1), o_b.reshape(T, -1)]
            w_o = bf(ev_w_out[j])
            w_outs = [w_o[:DA_WIDTH], w_o[DA_WIDTH:]]
        else:
            segs = ((0, D, "q"), (D, D, "k"), (2 * D, D, "plain"))
            q, k, v = _project(x2, bf(od_w_in[j]), tables, S, segs, (BF16, BF16, BF16))
            o = _dilated_attention(q.reshape(B, S, D), k.reshape(B, S, D), v.reshape(B, S, D))
            mixes = [o.reshape(T, D)]
            w_outs = [bf(od_w_out[j])]
        x2 = _layer_tail(x2, mixes, w_outs, ln1_g[l], ln1_b[l], bf(ffn_w1[l]), bf(ffn_w2[l]),
                         ln2_g[l], ln2_b[l], p[l].reshape(T, PLE_DIM), bf(ple_w_proj[l]),
                         bf(ple_w_gate[l]), ple_norm_g[l])
    return x2.reshape(B, S, D)
```

```python
import functools
import math

import jax
import jax.numpy as jnp
from jax import lax
from jax.experimental import pallas as pl
from jax.experimental.pallas import tpu as pltpu

D_MODEL = 1024
DEPTH = 2
PLE_DIM = 256

DA_HEADS = 4
DA_HEAD_DIM = 64
DA_V_DIM = 2 * DA_HEAD_DIM
DA_QK_WIDTH = DA_HEADS * 2 * DA_HEAD_DIM
DA_WIDTH = DA_HEADS * DA_V_DIM

HG_HEADS = 4
HG_KDIM = 128
HG_VDIM = 128
HG_KWIDTH = HG_HEADS * HG_KDIM
HG_WIDTH = HG_HEADS * HG_VDIM
HG_CHUNK = 64

DL_HEADS = 16
DL_HEAD_DIM = D_MODEL // DL_HEADS
DL_PAIRS = ((128, 1), (512, 4), (2048, 16))
DL_SPAN = 128

FFN_DIM = 4 * D_MODEL
ROPE_THETA = 500000.0
ROT_DIM = 16
ALPHA = (2 * DEPTH) ** 0.25
LN_EPS = 1e-5

LANES = 128
VMEM_LIMIT_BYTES = 56 * 1024 * 1024

F32 = jnp.float32
BF16 = jnp.bfloat16
NEG_INF = float("-inf")


def _params(sem, vmem=VMEM_LIMIT_BYTES):
    return pltpu.CompilerParams(dimension_semantics=sem, vmem_limit_bytes=vmem)


def _resident(shape):
    nd = len(shape)
    return pl.BlockSpec(shape, lambda *_: (0,) * nd, pipeline_mode=pl.Buffered(1))


def _sigmoid(x):
    return 1.0 / (1.0 + jnp.exp(-x))


def _layer_norm(x, g, b):
    mu = jnp.mean(x, axis=-1, keepdims=True)
    xc = x - mu
    var = jnp.mean(xc * xc, axis=-1, keepdims=True)
    return xc * lax.rsqrt(var + LN_EPS) * g + b


def _rms_norm(x, g):
    return x * lax.rsqrt(jnp.mean(x * x, axis=-1, keepdims=True) + LN_EPS) * g


def _rope_lane_tables(seq):
    half = ROT_DIM // 2
    inv = ROPE_THETA ** (-jnp.arange(0, ROT_DIM, 2, dtype=F32) / ROT_DIM)
    ang = jnp.arange(seq, dtype=F32)[:, None] * inv[None, :]
    cos, sin = jnp.cos(ang), jnp.sin(ang)
    rest = 64 - ROT_DIM
    zh = jnp.zeros((seq, half), F32)
    c = jnp.concatenate([cos, cos, jnp.ones((seq, rest), F32)], axis=-1)
    s1 = jnp.concatenate([zh, sin, jnp.zeros((seq, rest), F32)], axis=-1)
    s2 = jnp.concatenate([-sin, zh, jnp.zeros((seq, rest), F32)], axis=-1)
    tile = lambda t: jnp.concatenate([t, t], axis=-1)
    return tile(c), tile(s1), tile(s2)


def _proj_kernel(x_ref, w_ref, c_ref, s1_ref, s2_ref, *refs, segs, dils, col_chunk, tm):
    n_out = len(segs) * (1 + len(dils))
    out_refs, stage = refs[:n_out], refs[n_out:]
    xb = x_ref[...].astype(BF16)
    c, s1, s2 = c_ref[...], s1_ref[...], s2_ref[...]
    for n, (start, width, kind) in enumerate(segs):
        nat_ref = out_refs[n * (1 + len(dils))]
        dil_refs = out_refs[n * (1 + len(dils)) + 1:(n + 1) * (1 + len(dils))]
        for j in range(0, width, col_chunk):
            acc = jnp.dot(xb, w_ref[:, start + j:start + j + col_chunk],
                          preferred_element_type=F32)
            if kind != "plain":
                parts = []
                for g in range(col_chunk // LANES):
                    a = acc[:, g * LANES:(g + 1) * LANES]
                    r = (a * c + pltpu.roll(a, ROT_DIM // 2, 1) * s1
                         + pltpu.roll(a, LANES - ROT_DIM // 2, 1) * s2)
                    parts.append(r * (64 ** -0.5) if kind == "q" else r)
                acc = jnp.concatenate(parts, axis=1)
            nat_ref[:, j:j + col_chunk] = acc.astype(nat_ref.dtype)
            if dils:
                for g in range(col_chunk // LANES):
                    stage[0][g] = acc[:, g * LANES:(g + 1) * LANES]
                for d, ref in zip(dils, dil_refs):
                    for r in range(d):
                        for g in range(col_chunk // LANES):
                            ref[0, r, :, j + g * LANES:j + (g + 1) * LANES] = (
                                stage[0][g, pl.ds(r, tm // d, stride=d), :].astype(ref.dtype))


def _project(x2, w_bf, tables, seq, segs, out_dtypes, dils=(), tm=512, col_chunk=512):
    T, D = x2.shape
    N = w_bf.shape[1]
    nsb = seq // tm
    tab_spec = pl.BlockSpec((tm, LANES), lambda i: (i % nsb, 0))
    out_shape, out_specs = [], []
    for (_, w, _), dt in zip(segs, out_dtypes):
        out_shape.append(jax.ShapeDtypeStruct((T, w), dt))
        out_specs.append(pl.BlockSpec((tm, w), lambda i: (i, 0)))
        for d in dils:
            out_shape.append(jax.ShapeDtypeStruct((T // seq, d, seq // d, w), dt))
            out_specs.append(pl.BlockSpec((1, d, tm // d, w), lambda i: (i // nsb, 0, i % nsb, 0)))
    return pl.pallas_call(
        functools.partial(_proj_kernel, segs=segs, dils=dils, col_chunk=col_chunk, tm=tm),
        grid=(T // tm,),
        in_specs=[pl.BlockSpec((tm, D), lambda i: (i, 0)), _resident((D, N)),
                  tab_spec, tab_spec, tab_spec],
        out_specs=out_specs,
        out_shape=out_shape,
        scratch_shapes=[pltpu.VMEM((col_chunk // LANES, tm, LANES), F32)] if dils else [],
        compiler_params=_params(("parallel",)),
        name="in_proj",
    )(x2, w_bf, *tables)


def _diffattn_kernel(lam_ref, q_ref, k_ref, v_ref, g_ref, o_ref, acc_sc,
                     *, tq, unroll, lam_init):
    qi = pl.program_id(2)
    q = q_ref[0]
    lane = lax.broadcasted_iota(jnp.int32, (tq, LANES), 1)
    zero = jnp.zeros_like(q)
    qs = jnp.concatenate([jnp.where(lane < DA_HEAD_DIM, q, zero),
                          jnp.where(lane >= DA_HEAD_DIM, q, zero)], axis=0)
    acc_sc[...] = jnp.zeros(acc_sc.shape, F32)

    def scores(blk):
        start = pl.multiple_of(blk * tq, tq)
        return lax.dot_general(k_ref[0, pl.ds(start, tq), :], qs, (((1,), (1,)), ((), ())),
                               preferred_element_type=F32)

    def update(s, blk, carry, masked):
        m_prev, l_prev = carry
        if masked:
            krow = lax.broadcasted_iota(jnp.int32, (tq, 2 * tq), 0)
            col = lax.broadcasted_iota(jnp.int32, (tq, 2 * tq), 1)
            s = jnp.where(krow <= jnp.where(col >= tq, col - tq, col), s, NEG_INF)
        m_new = jnp.maximum(m_prev, jnp.max(s, axis=0, keepdims=True))
        alpha = jnp.exp(m_prev - m_new)
        p = jnp.exp(s - m_new)
        l_new = alpha * l_prev + jnp.sum(p, axis=0, keepdims=True)
        vj = v_ref[0, pl.ds(pl.multiple_of(blk * tq, tq), tq), :]
        pv = lax.dot_general(vj, p.astype(BF16), (((0,), (0,)), ((), ())),
                             preferred_element_type=F32)
        acc_sc[...] = alpha * acc_sc[...] + pv
        return m_new, l_new

    def run(first_blk, n, carry, diag_last):
        s_next = scores(first_blk)
        for u in range(n):
            s_cur = s_next
            if u + 1 < n:
                s_next = scores(first_blk + u + 1)
            carry = update(s_cur, first_blk + u, carry, diag_last and u == n - 1)
        return carry

    n_groups = qi // unroll
    carry = (jnp.full((1, 2 * tq), NEG_INF, F32), jnp.zeros((1, 2 * tq), F32))
    carry = lax.fori_loop(0, n_groups, lambda j, c: run(j * unroll, unroll, c, False), carry)
    tails = [functools.partial(run, n_groups * unroll, r + 1, diag_last=True)
             for r in range(unroll)]
    m, l = lax.switch(qi - n_groups * unroll, tails, carry)

    acc = acc_sc[...] / l
    o = acc[:, :tq] - lam_ref[0] * acc[:, tq:]
    o = o * lax.rsqrt(jnp.mean(o * o, axis=0, keepdims=True) + LN_EPS)
    o_ref[0] = (o.T * g_ref[...] * (1.0 - lam_init)).astype(o_ref.dtype)


def _diff_attention(q, k, v, lam, sub_g, lam_init, tq=256, unroll=4):
    B, S, _ = q.shape
    return pl.pallas_call(
        functools.partial(_diffattn_kernel, tq=tq, unroll=unroll, lam_init=lam_init),
        grid=(B, DA_HEADS, S // tq),
        in_specs=[
            pl.BlockSpec(memory_space=pltpu.SMEM),
            pl.BlockSpec((1, tq, LANES), lambda b, h, i: (b, i, h)),
            pl.BlockSpec((1, S, LANES), lambda b, h, i: (b, 0, h)),
            pl.BlockSpec((1, S, LANES), lambda b, h, i: (b, 0, h)),
            pl.BlockSpec((1, DA_V_DIM), lambda b, h, i: (0, 0)),
        ],
        out_specs=pl.BlockSpec((1, tq, LANES), lambda b, h, i: (b, i, h)),
        out_shape=jax.ShapeDtypeStruct((B, S, DA_WIDTH), BF16),
        scratch_shapes=[pltpu.VMEM((DA_V_DIM, 2 * tq), F32)],
        compiler_params=_params(("parallel", "parallel", "arbitrary")),
        name="diff_attn",
    )(lam, q, k, v, sub_g)


def _hgrn_kernel(g_ref, lb_ref, og_ref, o_ref, state_sc, *, chunks):
    @pl.when(pl.program_id(1) == 0)
    def _():
        state_sc[...] = jnp.zeros(state_sc.shape, F32)

    C, n = HG_CHUNK, chunks
    tc = n * C
    kw, vw = HG_KWIDTH, HG_WIDTH
    hq = g_ref[0, :, 0:kw]
    hf = g_ref[0, :, kw:2 * kw]
    vb = g_ref[0, :, 2 * kw:2 * kw + vw].astype(BF16)
    lb = lb_ref[...]
    q = hq * _sigmoid(hq)
    f = lb + (1.0 - lb) * _sigmoid(hf)
    kk = 1.0 - f
    ri = lax.broadcasted_iota(jnp.int32, (tc, tc), 0)
    ci = lax.broadcasted_iota(jnp.int32, (tc, tc), 1)
    tril = (ci <= ri) & (ci >= ri - ri % C)
    b = jnp.dot(tril.astype(F32), jnp.log(f), preferred_element_type=F32,
                precision=lax.Precision.HIGHEST).reshape(n, C, kw)
    b_last = b[:, C - 1:C, :]
    b_mid = b[:, C // 2 - 1:C // 2, :]
    q3 = q.reshape(n, C, kw)
    k3 = kk.reshape(n, C, kw)
    flat = lambda t: t.reshape(tc, kw).astype(BF16)
    qa = flat(q3 * jnp.exp(b - b_mid))
    ka = flat(k3 * jnp.exp(b_mid - b))
    q_out = flat(q3 * jnp.exp(b))
    k_st = flat(k3 * jnp.exp(b_last - b))
    decay = jnp.exp(b_last)

    outs = []
    for h in range(HG_HEADS):
        kc = slice(h * HG_KDIM, (h + 1) * HG_KDIM)
        vc = slice(h * HG_VDIM, (h + 1) * HG_VDIM)
        a = lax.dot_general(qa[:, kc], ka[:, kc], (((1,), (1,)), ((), ())),
                            preferred_element_type=F32)
        a = jnp.where(tril, a, 0.0).astype(BF16)
        outs.append(jnp.dot(a, vb[:, vc], preferred_element_type=F32))
    states = [state_sc[h] for h in range(HG_HEADS)]
    inter = [[None] * n for _ in range(HG_HEADS)]
    for c in range(n):
        rows = slice(c * C, (c + 1) * C)
        for h in range(HG_HEADS):
            kc = slice(h * HG_KDIM, (h + 1) * HG_KDIM)
            vc = slice(h * HG_VDIM, (h + 1) * HG_VDIM)
            st = states[h]
            inter[h][c] = lax.dot_general(q_out[rows, kc], st.astype(BF16),
                                          (((1,), (1,)), ((), ())), preferred_element_type=F32)
            upd = lax.dot_general(vb[rows, vc], k_st[rows, kc], (((0,), (0,)), ((), ())),
                                  preferred_element_type=F32)
            states[h] = decay[c, :, kc] * st + upd
    og = og_ref[...]
    for h in range(HG_HEADS):
        vc = slice(h * HG_VDIM, (h + 1) * HG_VDIM)
        state_sc[h] = states[h]
        hg = g_ref[0, :, 2 * kw + vw + h * HG_VDIM:2 * kw + vw + (h + 1) * HG_VDIM]
        o = outs[h] + jnp.concatenate(inter[h], axis=0)
        o = _rms_norm(o, og) * (hg * _sigmoid(hg))
        o_ref[0, :, vc] = o.astype(o_ref.dtype)


def _hgrn2(g, lb, out_g, tc=256):
    B, S, W = g.shape
    return pl.pallas_call(
        functools.partial(_hgrn_kernel, chunks=tc // HG_CHUNK),
        grid=(B, S // tc),
        in_specs=[pl.BlockSpec((1, tc, W), lambda b, i: (b, i, 0)),
                  pl.BlockSpec((1, HG_KWIDTH), lambda b, i: (0, 0)),
                  pl.BlockSpec((1, HG_VDIM), lambda b, i: (0, 0))],
        out_specs=pl.BlockSpec((1, tc, HG_WIDTH), lambda b, i: (b, i, 0)),
        out_shape=jax.ShapeDtypeStruct((B, S, HG_WIDTH), BF16),
        scratch_shapes=[pltpu.VMEM((HG_HEADS, HG_VDIM, HG_KDIM), F32)],
        compiler_params=_params(("parallel", "arbitrary")),
        name="hgrn2",
    )(g, lb, out_g)


def _dil_kernel(*refs, tl, n_sub, first, last):
    q_ref, kp_ref, kc_ref, vp_ref, vc_ref = refs[:5]
    refs = refs[5:]
    if not first:
        accp_ref, mlp_ref = refs[:2]
        refs = refs[2:]
    if last:
        o_ref = refs[0]
        refs = refs[1:]
    else:
        acc_ref, ml_ref = refs[:2]
        refs = refs[2:]
    i = pl.program_id(2)
    n_pairs = DL_HEADS // 2
    half = DL_HEAD_DIM

    if not first:
        acc_sc, ml_sc = refs
        quarter = n_sub * tl // 4
        for c in range(4):
            for hp in range(n_pairs):
                acc_sc[hp, pl.ds(c, quarter, stride=4), :] = (
                    accp_ref[0, c, 0, :, hp * LANES:(hp + 1) * LANES])
            ml_sc[pl.ds(c, quarter, stride=4), :] = mlp_ref[0, c, 0]

    lane = lax.broadcasted_iota(jnp.int32, (tl, LANES), 1)
    low = lane < half

    def pair_rows(x):
        return jnp.concatenate([jnp.broadcast_to(x[:, :tl], (half, tl)),
                                jnp.broadcast_to(x[:, tl:], (half, tl))], axis=0)

    def window(prev_ref, cur_ref, n, cols):
        if isinstance(n, int) and n == 0:
            return jnp.concatenate([prev_ref[0, 0, :, cols], cur_ref[0, 0, 0:tl, cols]], axis=0)
        return cur_ref[0, 0, pl.ds(pl.multiple_of((n - 1) * tl, tl), 2 * tl), cols]

    def rows_of(n):
        return pl.ds(pl.multiple_of(n * tl, tl), tl)

    def scores(n, hp):
        cols = slice(hp * LANES, (hp + 1) * LANES)
        qp = q_ref[0, 0, rows_of(n), cols]
        zero = jnp.zeros_like(qp)
        qs = jnp.concatenate([jnp.where(low, qp, zero), jnp.where(low, zero, qp)], axis=0)
        return lax.dot_general(window(kp_ref, kc_ref, n, cols), qs, (((1,), (1,)), ((), ())),
                               preferred_element_type=F32)

    def update(s, n, hp, ml_t):
        cols = slice(hp * LANES, (hp + 1) * LANES)
        key = lax.broadcasted_iota(jnp.int32, (2 * tl, 2 * tl), 0)
        qcol = lax.broadcasted_iota(jnp.int32, (2 * tl, 2 * tl), 1)
        dist = tl + jnp.where(qcol >= tl, qcol - tl, qcol) - key
        valid = (dist >= 0) & (dist <= DL_SPAN)
        if isinstance(n, int) and n == 0:
            valid = valid & ((key >= tl) | (i > 0))
        s = jnp.where(valid, s, NEG_INF)
        m_new = jnp.max(s, axis=0, keepdims=True)
        if not first:
            h0 = 2 * hp
            m_prev = jnp.concatenate([ml_t[h0:h0 + 1], ml_t[h0 + 1:h0 + 2]], axis=1)
            l_prev = jnp.concatenate([ml_t[DL_HEADS + h0:DL_HEADS + h0 + 1],
                                      ml_t[DL_HEADS + h0 + 1:DL_HEADS + h0 + 2]], axis=1)
            m_new = jnp.maximum(m_prev, m_new)
            alpha = jnp.exp(m_prev - m_new)
        p = jnp.exp(s - m_new)
        l_new = jnp.sum(p, axis=0, keepdims=True)
        pv = lax.dot_general(window(vp_ref, vc_ref, n, cols), p.astype(BF16),
                             (((0,), (0,)), ((), ())), preferred_element_type=F32)
        o_t = jnp.concatenate([pv[:half, :tl], pv[half:, tl:]], axis=0)
        if not first:
            l_new = l_new + alpha * l_prev
            o_t = o_t + pair_rows(alpha) * acc_sc[hp, rows_of(n), :].T
        if last:
            o_ref[0, 0, rows_of(n), cols] = (o_t * pair_rows(1.0 / l_new)).T.astype(o_ref.dtype)
        else:
            acc_ref[0, 0, rows_of(n), cols] = o_t.T
        return m_new, l_new

    def sub_block(n):
        ml_t = None if first else ml_sc[rows_of(n), :].T
        ms, ls = [], []
        s_next = scores(n, 0)
        for hp in range(n_pairs):
            s_cur = s_next
            if hp + 1 < n_pairs:
                s_next = scores(n, hp + 1)
            m_new, l_new = update(s_cur, n, hp, ml_t)
            ms += [m_new[:, :tl], m_new[:, tl:]]
            ls += [l_new[:, :tl], l_new[:, tl:]]
        if not last:
            pad = jnp.zeros((LANES - 2 * DL_HEADS, tl), F32)
            ml_ref[0, 0, rows_of(n), :] = jnp.concatenate(ms + ls + [pad], axis=0).T

    sub_block(0)
    if n_sub > 1:
        def body(n, carry):
            sub_block(n)
            return carry
        lax.fori_loop(1, n_sub, body, 0)


def _dilated_stage(q, k, v, dil, state, last, tl=DL_SPAN, max_sub=4):
    B, _, L, D = q.shape
    first = state is None
    n_sub = min(max_sub, L // tl)
    TL = n_sub * tl
    cur = lambda b, r, i: (b, r, i, 0)
    prev = lambda b, r, i: (b, r, jnp.maximum(i * n_sub - 1, 0), 0)
    tile = pl.BlockSpec((1, 1, TL, D), cur)
    halo = pl.BlockSpec((1, 1, tl, D), prev)
    in_specs = [tile, halo, tile, halo, tile]
    args = [q, k, k, v, v]
    scratch = []
    if not first:
        fine = lambda b, r, i: (b, 0, r, i, 0)
        in_specs += [pl.BlockSpec((1, 4, 1, TL // 4, D), fine),
                     pl.BlockSpec((1, 4, 1, TL // 4, LANES), fine)]
        args += [state[0].reshape(B, 4, dil, L // 4, D), state[1].reshape(B, 4, dil, L // 4, LANES)]
        scratch = [pltpu.VMEM((D // LANES, TL, LANES), F32), pltpu.VMEM((TL, LANES), F32)]
    if last:
        out_shape = [jax.ShapeDtypeStruct((B, dil, L, D), BF16)]
        out_specs = [tile]
    else:
        out_shape = [jax.ShapeDtypeStruct((B, dil, L, D), F32),
                     jax.ShapeDtypeStruct((B, dil, L, LANES), F32)]
        out_specs = [tile, pl.BlockSpec((1, 1, TL, LANES), cur)]
    outs = pl.pallas_call(
        functools.partial(_dil_kernel, tl=tl, n_sub=n_sub, first=first, last=last),
        grid=(B, dil, L // TL),
        in_specs=in_specs,
        out_specs=out_specs,
        out_shape=out_shape,
        scratch_shapes=scratch,
        compiler_params=_params(("parallel", "parallel", "arbitrary")),
        name=f"dilated_attn_d{dil}",
    )(*args)
    return outs[0] if last else tuple(outs)


def _dilated_attention(qkv_by_dil):
    dils = sorted((d for _, d in DL_PAIRS), reverse=True)
    assert all(w // d == DL_SPAN for w, d in DL_PAIRS)
    assert all(a == 4 * b for a, b in zip(dils, dils[1:])) and dils[-1] == 1
    state = None
    for d in dils:
        state = _dilated_stage(*qkv_by_dil[d], d, state, last=(d == 1))
    return state


def _tail_kernel(*refs, n_mix, ffn_chunk):
    x_ref = refs[0]
    mix_refs = refs[1:1 + n_mix]
    wo_refs = refs[1 + n_mix:1 + 2 * n_mix]
    (g1_ref, b1_ref, w1_ref, w2_ref, g2_ref, b2_ref,
     p_ref, wp_ref, wg_ref, pg_ref, o_ref) = refs[1 + 2 * n_mix:]
    mix = jnp.dot(mix_refs[0][...], wo_refs[0][...], preferred_element_type=F32)
    for m_ref, w_ref in zip(mix_refs[1:], wo_refs[1:]):
        mix = mix + jnp.dot(m_ref[...], w_ref[...], preferred_element_type=F32)
    x1 = _layer_norm(ALPHA * x_ref[...] + mix, g1_ref[...], b1_ref[...])
    x1b = x1.astype(BF16)
    ffn = None
    for c in range(0, FFN_DIM, ffn_chunk):
        hdn = jnp.dot(x1b, w1_ref[:, c:c + ffn_chunk], preferred_element_type=F32)
        hdn = jnp.square(jnp.maximum(hdn, 0.0)).astype(BF16)
        part = jnp.dot(hdn, w2_ref[c:c + ffn_chunk, :], preferred_element_type=F32)
        ffn = part if ffn is None else ffn + part
    x2 = _layer_norm(ALPHA * x1 + ffn, g2_ref[...], b2_ref[...])
    e = _rms_norm(jnp.dot(p_ref[...].astype(BF16), wp_ref[...], preferred_element_type=F32),
                  pg_ref[...])
    gate = _sigmoid(jnp.dot(x2.astype(BF16), wg_ref[...], preferred_element_type=F32))
    o_ref[...] = x2 + gate * e


def _layer_tail(x2, mixes, w_outs, g1, b1, w1, w2, g2, b2, p2, wp, wg, pg, tm=256,
                ffn_chunk=1024):
    T, D = x2.shape
    rows = lambda w: pl.BlockSpec((tm, w), lambda i: (i, 0))
    vec = lambda a: a.reshape(1, -1)
    in_specs = ([rows(D)] + [rows(m.shape[1]) for m in mixes]
                + [_resident(w.shape) for w in w_outs]
                + [_resident((1, D)), _resident((1, D)), _resident(w1.shape),
                   _resident(w2.shape), _resident((1, D)), _resident((1, D)),
                   rows(PLE_DIM), _resident(wp.shape), _resident(wg.shape),
                   _resident((1, D))])
    return pl.pallas_call(
        functools.partial(_tail_kernel, n_mix=len(mixes), ffn_chunk=ffn_chunk),
        grid=(T // tm,),
        in_specs=in_specs,
        out_specs=rows(D),
        out_shape=jax.ShapeDtypeStruct((T, D), F32),
        compiler_params=_params(("parallel",)),
        name="layer_tail",
    )(x2, *mixes, *w_outs, vec(g1), vec(b1), w1, w2, vec(g2), vec(b2), p2, wp, wg, vec(pg))


def kernel(x, p, ev_w_in, ev_w_out, da_lambda, da_subln_g, hg_lb_logits, hg_norm_g,
           od_w_in, od_w_out, ln1_g, ln1_b, ffn_w1, ffn_w2, ln2_g, ln2_b,
           ple_w_proj, ple_w_gate, ple_norm_g):
    B, S, D = x.shape
    T = B * S
    tables = _rope_lane_tables(S)
    lb_all = jnp.cumsum(jax.nn.softmax(hg_lb_logits.astype(F32), axis=0), axis=0)
    bf = lambda w: w.astype(BF16)
    x2 = x.reshape(T, D)
    for l in range(DEPTH):
        j = l // 2
        if l % 2 == 0:
            lam_init = 0.8 - 0.6 * math.exp(-0.3 * l)
            lp = da_lambda[j].astype(F32)
            lam = (jnp.exp(jnp.sum(lp[0] * lp[1])) - jnp.exp(jnp.sum(lp[2] * lp[3]))
                   + lam_init).reshape(1)
            segs = (("q", DA_QK_WIDTH), ("k", DA_QK_WIDTH), ("plain", DA_WIDTH),
                    ("plain", 2 * HG_KWIDTH + 2 * HG_WIDTH))
            starts = [sum(w for _, w in segs[:n]) for n in range(len(segs))]
            segs = tuple((s, w, kind) for s, (kind, w) in zip(starts, segs))
            q, k, v, g = _project(x2, bf(ev_w_in[j]), tables, S, segs, (BF16, BF16, BF16, F32))
            o_a = _diff_attention(q.reshape(B, S, -1), k.reshape(B, S, -1), v.reshape(B, S, -1),
                                  lam, da_subln_g[j].reshape(1, -1), lam_init)
            o_b = _hgrn2(g.reshape(B, S, -1), lb_all[j].reshape(1, -1),
                         hg_norm_g[j].reshape(1, -1))
            mixes = [o_a.reshape(T, -1), o_b.reshape(T, -1)]
            w_o = bf(ev_w_out[j])
            w_outs = [w_o[:DA_WIDTH], w_o[DA_WIDTH:]]
        else:
            segs = ((0, D, "q"), (D, D, "k"), (2 * D, D, "plain"))
            coarse = tuple(sorted(d for _, d in DL_PAIRS if d > 1))
            outs = _project(x2, bf(od_w_in[j]), tables, S, segs, (BF16, BF16, BF16), dils=coarse)
            per = 1 + len(coarse)
            qkv = {1: tuple(outs[n * per].reshape(B, 1, S, D) for n in range(3))}
            for m, d in enumerate(coarse):
                qkv[d] = tuple(outs[n * per + 1 + m] for n in range(3))
            o = _dilated_attention(qkv)
            mixes = [o.reshape(T, D)]
            w_outs = [bf(od_w_out[j])]
        x2 = _layer_tail(x2, mixes, w_outs, ln1_g[l], ln1_b[l], bf(ffn_w1[l]), bf(ffn_w2[l]),
                         ln2_g[l], ln2_b[l], p[l].reshape(T, PLE_DIM), bf(ple_w_proj[l]),
                         bf(ple_w_gate[l]), ple_norm_g[l])
    return x2.reshape(B, S, D)
```

```python
import functools
import math

import jax
import jax.numpy as jnp
from jax import lax
from jax.experimental import pallas as pl
from jax.experimental.pallas import tpu as pltpu

D_MODEL = 1024
DEPTH = 2
PLE_DIM = 256

DA_HEADS = 4
DA_HEAD_DIM = 64
DA_V_DIM = 2 * DA_HEAD_DIM
DA_QK_WIDTH = DA_HEADS * 2 * DA_HEAD_DIM
DA_WIDTH = DA_HEADS * DA_V_DIM
DA_TQ = 256
DA_VT_ROWS = DA_V_DIM + 16
LOG2_E = math.log2(math.e)

HG_HEADS = 4
HG_KDIM = 128
HG_VDIM = 128
HG_KWIDTH = HG_HEADS * HG_KDIM
HG_WIDTH = HG_HEADS * HG_VDIM
HG_CHUNK = 64

DL_HEADS = 16
DL_HEAD_DIM = D_MODEL // DL_HEADS
DL_PAIRS = ((128, 1), (512, 4), (2048, 16))
DL_SPAN = 128

FFN_DIM = 4 * D_MODEL
ROPE_THETA = 500000.0
ROT_DIM = 16
ALPHA = (2 * DEPTH) ** 0.25
LN_EPS = 1e-5

LANES = 128
VMEM_LIMIT_BYTES = 56 * 1024 * 1024

F32 = jnp.float32
BF16 = jnp.bfloat16
NEG_INF = float("-inf")


def _params(sem, vmem=VMEM_LIMIT_BYTES):
    return pltpu.CompilerParams(dimension_semantics=sem, vmem_limit_bytes=vmem)


def _resident(shape):
    nd = len(shape)
    return pl.BlockSpec(shape, lambda *_: (0,) * nd, pipeline_mode=pl.Buffered(1))


def _sigmoid(x):
    return 1.0 / (1.0 + jnp.exp(-x))


def _layer_norm(x, g, b):
    mu = jnp.mean(x, axis=-1, keepdims=True)
    xc = x - mu
    var = jnp.mean(xc * xc, axis=-1, keepdims=True)
    return xc * lax.rsqrt(var + LN_EPS) * g + b


def _rms_norm(x, g):
    return x * lax.rsqrt(jnp.mean(x * x, axis=-1, keepdims=True) + LN_EPS) * g


def _rope_lane_tables(seq):
    half = ROT_DIM // 2
    inv = ROPE_THETA ** (-jnp.arange(0, ROT_DIM, 2, dtype=F32) / ROT_DIM)
    ang = jnp.arange(seq, dtype=F32)[:, None] * inv[None, :]
    cos, sin = jnp.cos(ang), jnp.sin(ang)
    rest = 64 - ROT_DIM
    zh = jnp.zeros((seq, half), F32)
    c = jnp.concatenate([cos, cos, jnp.ones((seq, rest), F32)], axis=-1)
    s1 = jnp.concatenate([zh, sin, jnp.zeros((seq, rest), F32)], axis=-1)
    s2 = jnp.concatenate([-sin, zh, jnp.zeros((seq, rest), F32)], axis=-1)
    tile = lambda t: jnp.concatenate([t, t], axis=-1)
    return tile(c), tile(s1), tile(s2)


def _proj_kernel(x_ref, w_ref, c_ref, s1_ref, s2_ref, *refs, segs, dils, col_chunk, tm):
    n_out = len(segs) * (1 + len(dils))
    out_refs, stage = refs[:n_out], refs[n_out:]
    xb = x_ref[...].astype(BF16)
    c, s1, s2 = c_ref[...], s1_ref[...], s2_ref[...]
    for n, (start, width, kind) in enumerate(segs):
        nat_ref = out_refs[n * (1 + len(dils))]
        dil_refs = out_refs[n * (1 + len(dils)) + 1:(n + 1) * (1 + len(dils))]
        for j in range(0, width, col_chunk):
            acc = jnp.dot(xb, w_ref[:, start + j:start + j + col_chunk],
                          preferred_element_type=F32)
            if kind == "vt":
                for g in range(col_chunk // LANES):
                    head = (j + g * LANES) // DA_V_DIM
                    for sb in range(tm // DA_TQ):
                        t = acc[sb * DA_TQ:(sb + 1) * DA_TQ, g * LANES:(g + 1) * LANES].T
                        nat_ref[0, head, sb, 0:DA_V_DIM, :] = t.astype(nat_ref.dtype)
                        nat_ref[0, head, sb, DA_V_DIM:, :] = jnp.ones(
                            (DA_VT_ROWS - DA_V_DIM, DA_TQ), nat_ref.dtype)
                continue
            if kind != "plain":
                scale = {"q_log2": 64 ** -0.5 * LOG2_E, "k": None}[kind]
                parts = []
                for g in range(col_chunk // LANES):
                    a = acc[:, g * LANES:(g + 1) * LANES]
                    r = (a * c + pltpu.roll(a, ROT_DIM // 2, 1) * s1
                         + pltpu.roll(a, LANES - ROT_DIM // 2, 1) * s2)
                    parts.append(r if scale is None else r * scale)
                acc = jnp.concatenate(parts, axis=1)
            nat_ref[:, j:j + col_chunk] = acc.astype(nat_ref.dtype)
            if dils:
                for g in range(col_chunk // LANES):
                    stage[0][g] = acc[:, g * LANES:(g + 1) * LANES]
                for d, ref in zip(dils, dil_refs):
                    for r in range(d):
                        for g in range(col_chunk // LANES):
                            ref[0, r, :, j + g * LANES:j + (g + 1) * LANES] = (
                                stage[0][g, pl.ds(r, tm // d, stride=d), :].astype(ref.dtype))


def _project(x2, w_bf, tables, seq, segs, out_dtypes, dils=(), tm=512, col_chunk=512):
    T, D = x2.shape
    N = w_bf.shape[1]
    nsb = seq // tm
    tab_spec = pl.BlockSpec((tm, LANES), lambda i: (i % nsb, 0))
    out_shape, out_specs = [], []
    for (_, w, kind), dt in zip(segs, out_dtypes):
        if kind == "vt":
            heads, blocks = w // DA_V_DIM, tm // DA_TQ
            out_shape.append(jax.ShapeDtypeStruct(
                (T // seq, heads, seq // DA_TQ, DA_VT_ROWS, DA_TQ), dt))
            out_specs.append(pl.BlockSpec((1, heads, blocks, DA_VT_ROWS, DA_TQ),
                                          lambda i: (i // nsb, 0, i % nsb, 0, 0)))
            continue
        out_shape.append(jax.ShapeDtypeStruct((T, w), dt))
        out_specs.append(pl.BlockSpec((tm, w), lambda i: (i, 0)))
        for d in dils:
            out_shape.append(jax.ShapeDtypeStruct((T // seq, d, seq // d, w), dt))
            out_specs.append(pl.BlockSpec((1, d, tm // d, w), lambda i: (i // nsb, 0, i % nsb, 0)))
    return pl.pallas_call(
        functools.partial(_proj_kernel, segs=segs, dils=dils, col_chunk=col_chunk, tm=tm),
        grid=(T // tm,),
        in_specs=[pl.BlockSpec((tm, D), lambda i: (i, 0)), _resident((D, N)),
                  tab_spec, tab_spec, tab_spec],
        out_specs=out_specs,
        out_shape=out_shape,
        scratch_shapes=[pltpu.VMEM((col_chunk // LANES, tm, LANES), F32)] if dils else [],
        compiler_params=_params(("parallel",)),
        name="in_proj",
    )(x2, w_bf, *tables)


def _diffattn_kernel(lam_ref, q_ref, k_ref, vt_ref, g_ref, o_ref, acc_sc,
                     *, tq, tk, unroll, ahead, lam_init):
    qi = pl.program_id(2)
    n_diag = tq // tk
    q = q_ref[0]
    lane = lax.broadcasted_iota(jnp.int32, (tq, LANES), 1)
    zero = jnp.zeros_like(q)
    qs = jnp.concatenate([jnp.where(lane < DA_HEAD_DIM, q, zero),
                          jnp.where(lane >= DA_HEAD_DIM, q, zero)], axis=0)
    acc_sc[...] = jnp.zeros(acc_sc.shape, F32)

    def scores(blk):
        start = pl.multiple_of(blk * tk, tk)
        return lax.dot_general(k_ref[0, pl.ds(start, tk), :], qs, (((1,), (1,)), ((), ())),
                               preferred_element_type=F32)

    def update(s, blk, m_prev, diag):
        if diag is not None:
            krow = diag * tk + lax.broadcasted_iota(jnp.int32, (tk, 2 * tq), 0)
            col = lax.broadcasted_iota(jnp.int32, (tk, 2 * tq), 1)
            s = jnp.where(krow <= jnp.where(col >= tq, col - tq, col), s, NEG_INF)
        m_new = jnp.maximum(m_prev, jnp.max(s, axis=0, keepdims=True))
        alpha = jnp.exp2(m_prev - m_new)
        p = jnp.exp2(s - m_new).astype(BF16)
        pv = jnp.dot(vt_ref[0, 0, blk], p, preferred_element_type=F32)
        acc_sc[...] = alpha * acc_sc[...] + pv
        return m_new

    def run(first_blk, n_plain, m, with_diag):
        n = n_plain + (n_diag if with_diag else 0)
        pending = [scores(first_blk + u) for u in range(min(ahead, n))]
        for u in range(n):
            if u + ahead < n:
                pending.append(scores(first_blk + u + ahead))
            m = update(pending[u], first_blk + u, m, u - n_plain if u >= n_plain else None)
        return m

    n_before = qi * n_diag
    n_groups = n_before // unroll
    m = jnp.full((1, 2 * tq), NEG_INF, F32)
    m = lax.fori_loop(0, n_groups, lambda j, c: run(j * unroll, unroll, c, False), m)
    step = math.gcd(n_diag, unroll)
    tails = [functools.partial(run, n_groups * unroll, r, with_diag=True)
             for r in range(0, unroll, step)]
    lax.switch((n_before - n_groups * unroll) // step, tails, m)

    acc = acc_sc[0:DA_V_DIM, :] / acc_sc[DA_V_DIM:DA_V_DIM + 1, :]
    o = acc[:, :tq] - lam_ref[0] * acc[:, tq:]
    o = o * lax.rsqrt(jnp.mean(o * o, axis=0, keepdims=True) + LN_EPS)
    o_ref[0] = (o.T * g_ref[...] * (1.0 - lam_init)).astype(o_ref.dtype)


def _diff_attention(q, k, vt, lam, sub_g, lam_init, tq=512, unroll=8, ahead=2):
    B, S, _ = q.shape
    tk = DA_TQ
    return pl.pallas_call(
        functools.partial(_diffattn_kernel, tq=tq, tk=tk, unroll=unroll, ahead=ahead,
                          lam_init=lam_init),
        grid=(B, DA_HEADS, S // tq),
        in_specs=[
            pl.BlockSpec(memory_space=pltpu.SMEM),
            pl.BlockSpec((1, tq, LANES), lambda b, h, i: (b, i, h)),
            pl.BlockSpec((1, S, LANES), lambda b, h, i: (b, 0, h)),
            pl.BlockSpec((1, 1, S // tk, DA_VT_ROWS, tk), lambda b, h, i: (b, h, 0, 0, 0)),
            pl.BlockSpec((1, DA_V_DIM), lambda b, h, i: (0, 0)),
        ],
        out_specs=pl.BlockSpec((1, tq, LANES), lambda b, h, i: (b, i, h)),
        out_shape=jax.ShapeDtypeStruct((B, S, DA_WIDTH), BF16),
        scratch_shapes=[pltpu.VMEM((DA_VT_ROWS, 2 * tq), F32)],
        compiler_params=_params(("parallel", "parallel", "arbitrary")),
        name="diff_attn",
    )(lam, q, k, vt, sub_g)


def _hgrn_kernel(g_ref, lb_ref, og_ref, o_ref, state_sc, *, chunks):
    @pl.when(pl.program_id(1) == 0)
    def _():
        state_sc[...] = jnp.zeros(state_sc.shape, F32)

    C, n = HG_CHUNK, chunks
    tc = n * C
    kw, vw = HG_KWIDTH, HG_WIDTH
    hq = g_ref[0, :, 0:kw]
    hf = g_ref[0, :, kw:2 * kw]
    vb = g_ref[0, :, 2 * kw:2 * kw + vw].astype(BF16)
    lb = lb_ref[...]
    q = hq * _sigmoid(hq)
    f = lb + (1.0 - lb) * _sigmoid(hf)
    kk = 1.0 - f
    ri = lax.broadcasted_iota(jnp.int32, (tc, tc), 0)
    ci = lax.broadcasted_iota(jnp.int32, (tc, tc), 1)
    tril = (ci <= ri) & (ci >= ri - ri % C)
    b = jnp.dot(tril.astype(F32), jnp.log(f), preferred_element_type=F32,
                precision=lax.Precision.HIGHEST).reshape(n, C, kw)
    b_last = b[:, C - 1:C, :]
    b_mid = b[:, C // 2 - 1:C // 2, :]
    q3 = q.reshape(n, C, kw)
    k3 = kk.reshape(n, C, kw)
    flat = lambda t: t.reshape(tc, kw).astype(BF16)
    qa = flat(q3 * jnp.exp(b - b_mid))
    ka = flat(k3 * jnp.exp(b_mid - b))
    q_out = flat(q3 * jnp.exp(b))
    k_st = flat(k3 * jnp.exp(b_last - b))
    decay = jnp.exp(b_last)

    outs = []
    for h in range(HG_HEADS):
        kc = slice(h * HG_KDIM, (h + 1) * HG_KDIM)
        vc = slice(h * HG_VDIM, (h + 1) * HG_VDIM)
        a = lax.dot_general(qa[:, kc], ka[:, kc], (((1,), (1,)), ((), ())),
                            preferred_element_type=F32)
        a = jnp.where(tril, a, 0.0).astype(BF16)
        outs.append(jnp.dot(a, vb[:, vc], preferred_element_type=F32))
    states = [state_sc[h] for h in range(HG_HEADS)]
    inter = [[None] * n for _ in range(HG_HEADS)]
    for c in range(n):
        rows = slice(c * C, (c + 1) * C)
        for h in range(HG_HEADS):
            kc = slice(h * HG_KDIM, (h + 1) * HG_KDIM)
            vc = slice(h * HG_VDIM, (h + 1) * HG_VDIM)
            st = states[h]
            inter[h][c] = lax.dot_general(q_out[rows, kc], st.astype(BF16),
                                          (((1,), (1,)), ((), ())), preferred_element_type=F32)
            upd = lax.dot_general(vb[rows, vc], k_st[rows, kc], (((0,), (0,)), ((), ())),
                                  preferred_element_type=F32)
            states[h] = decay[c, :, kc] * st + upd
    og = og_ref[...]
    for h in range(HG_HEADS):
        vc = slice(h * HG_VDIM, (h + 1) * HG_VDIM)
        state_sc[h] = states[h]
        hg = g_ref[0, :, 2 * kw + vw + h * HG_VDIM:2 * kw + vw + (h + 1) * HG_VDIM]
        o = outs[h] + jnp.concatenate(inter[h], axis=0)
        o = _rms_norm(o, og) * (hg * _sigmoid(hg))
        o_ref[0, :, vc] = o.astype(o_ref.dtype)


def _hgrn2(g, lb, out_g, tc=256):
    B, S, W = g.shape
    return pl.pallas_call(
        functools.partial(_hgrn_kernel, chunks=tc // HG_CHUNK),
        grid=(B, S // tc),
        in_specs=[pl.BlockSpec((1, tc, W), lambda b, i: (b, i, 0)),
                  pl.BlockSpec((1, HG_KWIDTH), lambda b, i: (0, 0)),
                  pl.BlockSpec((1, HG_VDIM), lambda b, i: (0, 0))],
        out_specs=pl.BlockSpec((1, tc, HG_WIDTH), lambda b, i: (b, i, 0)),
        out_shape=jax.ShapeDtypeStruct((B, S, HG_WIDTH), BF16),
        scratch_shapes=[pltpu.VMEM((HG_HEADS, HG_VDIM, HG_KDIM), F32)],
        compiler_params=_params(("parallel", "arbitrary")),
        name="hgrn2",
    )(g, lb, out_g)


def _dil_kernel(*refs, tl, n_sub, ahead, first, last):
    q_ref, kp_ref, kc_ref, vp_ref, vc_ref = refs[:5]
    refs = refs[5:]
    if not first:
        accp_ref, mlp_ref = refs[:2]
        refs = refs[2:]
    if last:
        o_ref = refs[0]
        refs = refs[1:]
    else:
        acc_ref, ml_ref = refs[:2]
        refs = refs[2:]
    i = pl.program_id(2)
    n_pairs = DL_HEADS // 2
    half = DL_HEAD_DIM

    if not first:
        acc_sc, ml_sc = refs
        quarter = n_sub * tl // 4
        for c in range(4):
            for hp in range(n_pairs):
                acc_sc[hp, pl.ds(c, quarter, stride=4), :] = (
                    accp_ref[0, c, 0, :, hp * LANES:(hp + 1) * LANES])
            ml_sc[pl.ds(c, quarter, stride=4), :] = mlp_ref[0, c, 0]

    lane = lax.broadcasted_iota(jnp.int32, (tl, LANES), 1)
    low = lane < half

    def pair_rows(x):
        return jnp.concatenate([jnp.broadcast_to(x[:, :tl], (half, tl)),
                                jnp.broadcast_to(x[:, tl:], (half, tl))], axis=0)

    def window(prev_ref, cur_ref, n, cols):
        if isinstance(n, int) and n == 0:
            return jnp.concatenate([prev_ref[0, 0, :, cols], cur_ref[0, 0, 0:tl, cols]], axis=0)
        return cur_ref[0, 0, pl.ds(pl.multiple_of((n - 1) * tl, tl), 2 * tl), cols]

    def rows_of(n):
        return pl.ds(pl.multiple_of(n * tl, tl), tl)

    def scores(n, hp):
        cols = slice(hp * LANES, (hp + 1) * LANES)
        qp = q_ref[0, 0, rows_of(n), cols]
        zero = jnp.zeros_like(qp)
        qs = jnp.concatenate([jnp.where(low, qp, zero), jnp.where(low, zero, qp)], axis=0)
        return lax.dot_general(window(kp_ref, kc_ref, n, cols), qs, (((1,), (1,)), ((), ())),
                               preferred_element_type=F32)

    def update(s, n, hp, ml_t):
        cols = slice(hp * LANES, (hp + 1) * LANES)
        key = lax.broadcasted_iota(jnp.int32, (2 * tl, 2 * tl), 0)
        qcol = lax.broadcasted_iota(jnp.int32, (2 * tl, 2 * tl), 1)
        dist = tl + jnp.where(qcol >= tl, qcol - tl, qcol) - key
        valid = (dist >= 0) & (dist <= DL_SPAN)
        if isinstance(n, int) and n == 0:
            valid = valid & ((key >= tl) | (i > 0))
        s = jnp.where(valid, s, NEG_INF)
        m_new = jnp.max(s, axis=0, keepdims=True)
        if not first:
            h0 = 2 * hp
            m_prev = jnp.concatenate([ml_t[h0:h0 + 1], ml_t[h0 + 1:h0 + 2]], axis=1)
            l_prev = jnp.concatenate([ml_t[DL_HEADS + h0:DL_HEADS + h0 + 1],
                                      ml_t[DL_HEADS + h0 + 1:DL_HEADS + h0 + 2]], axis=1)
            m_new = jnp.maximum(m_prev, m_new)
            alpha = jnp.exp2(m_prev - m_new)
        p = jnp.exp2(s - m_new)
        l_new = jnp.sum(p, axis=0, keepdims=True)
        pv = lax.dot_general(window(vp_ref, vc_ref, n, cols), p.astype(BF16),
                             (((0,), (0,)), ((), ())), preferred_element_type=F32)
        o_t = jnp.concatenate([pv[:half, :tl], pv[half:, tl:]], axis=0)
        if not first:
            l_new = l_new + alpha * l_prev
            o_t = o_t + pair_rows(alpha) * acc_sc[hp, rows_of(n), :].T
        if last:
            o_ref[0, 0, rows_of(n), cols] = (o_t * pair_rows(1.0 / l_new)).T.astype(o_ref.dtype)
        else:
            acc_ref[0, 0, rows_of(n), cols] = o_t.T
        return m_new, l_new

    def sub_block(n):
        ml_t = None if first else ml_sc[rows_of(n), :].T
        ms, ls = [], []
        pending = [scores(n, hp) for hp in range(ahead)]
        for hp in range(n_pairs):
            if hp + ahead < n_pairs:
                pending.append(scores(n, hp + ahead))
            m_new, l_new = update(pending[hp], n, hp, ml_t)
            ms += [m_new[:, :tl], m_new[:, tl:]]
            ls += [l_new[:, :tl], l_new[:, tl:]]
        if not last:
            pad = jnp.zeros((LANES - 2 * DL_HEADS, tl), F32)
            ml_ref[0, 0, rows_of(n), :] = jnp.concatenate(ms + ls + [pad], axis=0).T

    sub_block(0)
    if n_sub > 1:
        def body(n, carry):
            sub_block(n)
            return carry
        lax.fori_loop(1, n_sub, body, 0)


def _dilated_stage(q, k, v, dil, state, last, tl=DL_SPAN, max_sub=4, ahead=2):
    B, _, L, D = q.shape
    first = state is None
    n_sub = min(max_sub, L // tl)
    TL = n_sub * tl
    cur = lambda b, r, i: (b, r, i, 0)
    prev = lambda b, r, i: (b, r, jnp.maximum(i * n_sub - 1, 0), 0)
    tile = pl.BlockSpec((1, 1, TL, D), cur)
    halo = pl.BlockSpec((1, 1, tl, D), prev)
    in_specs = [tile, halo, tile, halo, tile]
    args = [q, k, k, v, v]
    scratch = []
    if not first:
        fine = lambda b, r, i: (b, 0, r, i, 0)
        in_specs += [pl.BlockSpec((1, 4, 1, TL // 4, D), fine),
                     pl.BlockSpec((1, 4, 1, TL // 4, LANES), fine)]
        args += [state[0].reshape(B, 4, dil, L // 4, D), state[1].reshape(B, 4, dil, L // 4, LANES)]
        scratch = [pltpu.VMEM((D // LANES, TL, LANES), F32), pltpu.VMEM((TL, LANES), F32)]
    if last:
        out_shape = [jax.ShapeDtypeStruct((B, dil, L, D), BF16)]
        out_specs = [tile]
    else:
        out_shape = [jax.ShapeDtypeStruct((B, dil, L, D), F32),
                     jax.ShapeDtypeStruct((B, dil, L, LANES), F32)]
        out_specs = [tile, pl.BlockSpec((1, 1, TL, LANES), cur)]
    outs = pl.pallas_call(
        functools.partial(_dil_kernel, tl=tl, n_sub=n_sub, ahead=ahead, first=first, last=last),
        grid=(B, dil, L // TL),
        in_specs=in_specs,
        out_specs=out_specs,
        out_shape=out_shape,
        scratch_shapes=scratch,
        compiler_params=_params(("parallel", "parallel", "arbitrary")),
        name=f"dilated_attn_d{dil}",
    )(*args)
    return outs[0] if last else tuple(outs)


def _dilated_attention(qkv_by_dil):
    dils = sorted((d for _, d in DL_PAIRS), reverse=True)
    assert all(w // d == DL_SPAN for w, d in DL_PAIRS)
    assert all(a == 4 * b for a, b in zip(dils, dils[1:])) and dils[-1] == 1
    state = None
    for d in dils:
        state = _dilated_stage(*qkv_by_dil[d], d, state, last=(d == 1))
    return state


def _tail_kernel(*refs, n_mix, ffn_chunk):
    x_ref = refs[0]
    mix_refs = refs[1:1 + n_mix]
    wo_refs = refs[1 + n_mix:1 + 2 * n_mix]
    (g1_ref, b1_ref, w1_ref, w2_ref, g2_ref, b2_ref,
     p_ref, wp_ref, wg_ref, pg_ref, o_ref) = refs[1 + 2 * n_mix:]
    mix = jnp.dot(mix_refs[0][...], wo_refs[0][...], preferred_element_type=F32)
    for m_ref, w_ref in zip(mix_refs[1:], wo_refs[1:]):
        mix = mix + jnp.dot(m_ref[...], w_ref[...], preferred_element_type=F32)
    x1 = _layer_norm(ALPHA * x_ref[...] + mix, g1_ref[...], b1_ref[...])
    x1b = x1.astype(BF16)
    ffn = None
    for c in range(0, FFN_DIM, ffn_chunk):
        hdn = jnp.dot(x1b, w1_ref[:, c:c + ffn_chunk], preferred_element_type=F32)
        hdn = jnp.square(jnp.maximum(hdn, 0.0)).astype(BF16)
        part = jnp.dot(hdn, w2_ref[c:c + ffn_chunk, :], preferred_element_type=F32)
        ffn = part if ffn is None else ffn + part
    x2 = _layer_norm(ALPHA * x1 + ffn, g2_ref[...], b2_ref[...])
    e = _rms_norm(jnp.dot(p_ref[...].astype(BF16), wp_ref[...], preferred_element_type=F32),
                  pg_ref[...])
    gate = _sigmoid(jnp.dot(x2.astype(BF16), wg_ref[...], preferred_element_type=F32))
    o_ref[...] = x2 + gate * e


def _layer_tail(x2, mixes, w_outs, g1, b1, w1, w2, g2, b2, p2, wp, wg, pg, tm=256,
                ffn_chunk=1024):
    T, D = x2.shape
    rows = lambda w: pl.BlockSpec((tm, w), lambda i: (i, 0))
    vec = lambda a: a.reshape(1, -1)
    in_specs = ([rows(D)] + [rows(m.shape[1]) for m in mixes]
                + [_resident(w.shape) for w in w_outs]
                + [_resident((1, D)), _resident((1, D)), _resident(w1.shape),
                   _resident(w2.shape), _resident((1, D)), _resident((1, D)),
                   rows(PLE_DIM), _resident(wp.shape), _resident(wg.shape),
                   _resident((1, D))])
    return pl.pallas_call(
        functools.partial(_tail_kernel, n_mix=len(mixes), ffn_chunk=ffn_chunk),
        grid=(T // tm,),
        in_specs=in_specs,
        out_specs=rows(D),
        out_shape=jax.ShapeDtypeStruct((T, D), F32),
        compiler_params=_params(("parallel",)),
        name="layer_tail",
    )(x2, *mixes, *w_outs, vec(g1), vec(b1), w1, w2, vec(g2), vec(b2), p2, wp, wg, vec(pg))


def kernel(x, p, ev_w_in, ev_w_out, da_lambda, da_subln_g, hg_lb_logits, hg_norm_g,
           od_w_in, od_w_out, ln1_g, ln1_b, ffn_w1, ffn_w2, ln2_g, ln2_b,
           ple_w_proj, ple_w_gate, ple_norm_g):
    B, S, D = x.shape
    T = B * S
    tables = _rope_lane_tables(S)
    lb_all = jnp.cumsum(jax.nn.softmax(hg_lb_logits.astype(F32), axis=0), axis=0)
    bf = lambda w: w.astype(BF16)
    x2 = x.reshape(T, D)
    for l in range(DEPTH):
        j = l // 2
        if l % 2 == 0:
            lam_init = 0.8 - 0.6 * math.exp(-0.3 * l)
            lp = da_lambda[j].astype(F32)
            lam = (jnp.exp(jnp.sum(lp[0] * lp[1])) - jnp.exp(jnp.sum(lp[2] * lp[3]))
                   + lam_init).reshape(1)
            segs = (("q_log2", DA_QK_WIDTH), ("k", DA_QK_WIDTH), ("vt", DA_WIDTH),
                    ("plain", 2 * HG_KWIDTH + 2 * HG_WIDTH))
            starts = [sum(w for _, w in segs[:n]) for n in range(len(segs))]
            segs = tuple((s, w, kind) for s, (kind, w) in zip(starts, segs))
            q, k, vt, g = _project(x2, bf(ev_w_in[j]), tables, S, segs, (BF16, BF16, BF16, F32))
            o_a = _diff_attention(q.reshape(B, S, -1), k.reshape(B, S, -1), vt,
                                  lam, da_subln_g[j].reshape(1, -1), lam_init)
            o_b = _hgrn2(g.reshape(B, S, -1), lb_all[j].reshape(1, -1),
                         hg_norm_g[j].reshape(1, -1))
            mixes = [o_a.reshape(T, -1), o_b.reshape(T, -1)]
            w_o = bf(ev_w_out[j])
            w_outs = [w_o[:DA_WIDTH], w_o[DA_WIDTH:]]
        else:
            segs = ((0, D, "q_log2"), (D, D, "k"), (2 * D, D, "plain"))
            coarse = tuple(sorted(d for _, d in DL_PAIRS if d > 1))
            outs = _project(x2, bf(od_w_in[j]), tables, S, segs, (BF16, BF16, BF16), dils=coarse)
            per = 1 + len(coarse)
            qkv = {1: tuple(outs[n * per].reshape(B, 1, S, D) for n in range(3))}
            for m, d in enumerate(coarse):
                qkv[d] = tuple(outs[n * per + 1 + m] for n in range(3))
            o = _dilated_attention(qkv)
            mixes = [o.reshape(T, D)]
            w_outs = [bf(od_w_out[j])]
        x2 = _layer_tail(x2, mixes, w_outs, ln1_g[l], ln1_b[l], bf(ffn_w1[l]), bf(ffn_w2[l]),
                         ln2_g[l], ln2_b[l], p[l].reshape(T, PLE_DIM), bf(ple_w_proj[l]),
                         bf(ple_w_gate[l]), ple_norm_g[l])
    return x2.reshape(B, S, D)
```

```python
import functools
import math

import jax
import jax.numpy as jnp
from jax import lax
from jax.experimental import pallas as pl
from jax.experimental.pallas import tpu as pltpu

D_MODEL = 1024
DEPTH = 2
PLE_DIM = 256

DA_HEADS = 4
DA_HEAD_DIM = 64
DA_V_DIM = 2 * DA_HEAD_DIM
DA_QK_WIDTH = DA_HEADS * 2 * DA_HEAD_DIM
DA_WIDTH = DA_HEADS * DA_V_DIM
DA_TQ = 256
DA_VT_ROWS = DA_V_DIM + 16
LOG2_E = math.log2(math.e)

HG_HEADS = 4
HG_KDIM = 128
HG_VDIM = 128
HG_KWIDTH = HG_HEADS * HG_KDIM
HG_WIDTH = HG_HEADS * HG_VDIM
HG_CHUNK = 64

DL_HEADS = 16
DL_HEAD_DIM = D_MODEL // DL_HEADS
DL_PAIRS = ((128, 1), (512, 4), (2048, 16))
DL_SPAN = 128

FFN_DIM = 4 * D_MODEL
ROPE_THETA = 500000.0
ROT_DIM = 16
ALPHA = (2 * DEPTH) ** 0.25
LN_EPS = 1e-5

LANES = 128
VMEM_LIMIT_BYTES = 56 * 1024 * 1024

F32 = jnp.float32
BF16 = jnp.bfloat16
NEG_INF = float("-inf")


def _params(sem, vmem=VMEM_LIMIT_BYTES):
    return pltpu.CompilerParams(dimension_semantics=sem, vmem_limit_bytes=vmem)


def _resident(shape):
    nd = len(shape)
    return pl.BlockSpec(shape, lambda *_: (0,) * nd, pipeline_mode=pl.Buffered(1))


def _sigmoid(x):
    return 1.0 / (1.0 + jnp.exp(-x))


def _layer_norm(x, g, b):
    mu = jnp.mean(x, axis=-1, keepdims=True)
    xc = x - mu
    var = jnp.mean(xc * xc, axis=-1, keepdims=True)
    return xc * lax.rsqrt(var + LN_EPS) * g + b


def _rms_norm(x, g):
    return x * lax.rsqrt(jnp.mean(x * x, axis=-1, keepdims=True) + LN_EPS) * g


def _rope_lane_tables(seq):
    half = ROT_DIM // 2
    inv = ROPE_THETA ** (-jnp.arange(0, ROT_DIM, 2, dtype=F32) / ROT_DIM)
    lane = jnp.arange(LANES) % 64
    inv_lane = jnp.where(lane < ROT_DIM, inv[lane % half], 0.0)
    ang = jnp.arange(seq, dtype=F32)[:, None] * inv_lane[None, :]
    cos, sin = jnp.cos(ang), jnp.sin(ang)
    s1 = jnp.where((lane >= half) & (lane < ROT_DIM), sin, 0.0)
    s2 = jnp.where(lane < half, -sin, 0.0)
    return cos, s1, s2


def _proj_kernel(x_ref, w_ref, c_ref, s1_ref, s2_ref, *refs, segs, dils, col_chunk, tm):
    n_out = len(segs) * (1 + len(dils))
    out_refs, stage = refs[:n_out], refs[n_out:]
    xb = x_ref[...].astype(BF16)
    c, s1, s2 = c_ref[...], s1_ref[...], s2_ref[...]
    for n, (start, width, kind) in enumerate(segs):
        nat_ref = out_refs[n * (1 + len(dils))]
        dil_refs = out_refs[n * (1 + len(dils)) + 1:(n + 1) * (1 + len(dils))]
        for j in range(0, width, col_chunk):
            acc = jnp.dot(xb, w_ref[:, start + j:start + j + col_chunk],
                          preferred_element_type=F32)
            if kind == "vt":
                for g in range(col_chunk // LANES):
                    head = (j + g * LANES) // DA_V_DIM
                    for sb in range(tm // DA_TQ):
                        t = acc[sb * DA_TQ:(sb + 1) * DA_TQ, g * LANES:(g + 1) * LANES].T
                        nat_ref[0, head, sb, 0:DA_V_DIM, :] = t.astype(nat_ref.dtype)
                        nat_ref[0, head, sb, DA_V_DIM:, :] = jnp.ones(
                            (DA_VT_ROWS - DA_V_DIM, DA_TQ), nat_ref.dtype)
                continue
            if kind != "plain":
                scale = {"q_log2": 64 ** -0.5 * LOG2_E, "k": None}[kind]
                parts = []
                for g in range(col_chunk // LANES):
                    a = acc[:, g * LANES:(g + 1) * LANES]
                    r = (a * c + pltpu.roll(a, ROT_DIM // 2, 1) * s1
                         + pltpu.roll(a, LANES - ROT_DIM // 2, 1) * s2)
                    parts.append(r if scale is None else r * scale)
                acc = jnp.concatenate(parts, axis=1)
            nat_ref[:, j:j + col_chunk] = acc.astype(nat_ref.dtype)
            if dils:
                for g in range(col_chunk // LANES):
                    cols = slice(j + g * LANES, j + (g + 1) * LANES)
                    src, d_prev = stage[0], 1
                    src[g] = acc[:, g * LANES:(g + 1) * LANES]
                    for n_d, (d, ref) in enumerate(zip(dils, dil_refs)):
                        ratio, rows = d // d_prev, tm // d
                        dst = stage[(n_d + 1) % 2] if n_d + 1 < len(dils) else None
                        for r_prev in range(d_prev):
                            for c_new in range(ratio):
                                r = r_prev + d_prev * c_new
                                piece = src[g, pl.ds(r_prev * (tm // d_prev) + c_new, rows,
                                                     stride=ratio), :]
                                ref[0, r, :, cols] = piece.astype(ref.dtype)
                                if dst is not None:
                                    dst[g, r * rows:(r + 1) * rows, :] = piece
                        src, d_prev = dst, d


def _project(x2, w_bf, tables, seq, segs, out_dtypes, dils=(), tm=512, col_chunk=512):
    T, D = x2.shape
    N = w_bf.shape[1]
    nsb = seq // tm
    assert all(b % a == 0 for a, b in zip((1,) + tuple(dils), dils))
    tab_spec = pl.BlockSpec((tm, LANES), lambda i: (i % nsb, 0))
    out_shape, out_specs = [], []
    for (_, w, kind), dt in zip(segs, out_dtypes):
        if kind == "vt":
            heads, blocks = w // DA_V_DIM, tm // DA_TQ
            out_shape.append(jax.ShapeDtypeStruct(
                (T // seq, heads, seq // DA_TQ, DA_VT_ROWS, DA_TQ), dt))
            out_specs.append(pl.BlockSpec((1, heads, blocks, DA_VT_ROWS, DA_TQ),
                                          lambda i: (i // nsb, 0, i % nsb, 0, 0)))
            continue
        out_shape.append(jax.ShapeDtypeStruct((T, w), dt))
        out_specs.append(pl.BlockSpec((tm, w), lambda i: (i, 0)))
        for d in dils:
            out_shape.append(jax.ShapeDtypeStruct((T // seq, d, seq // d, w), dt))
            out_specs.append(pl.BlockSpec((1, d, tm // d, w), lambda i: (i // nsb, 0, i % nsb, 0)))
    return pl.pallas_call(
        functools.partial(_proj_kernel, segs=segs, dils=dils, col_chunk=col_chunk, tm=tm),
        grid=(T // tm,),
        in_specs=[pl.BlockSpec((tm, D), lambda i: (i, 0)), _resident((D, N)),
                  tab_spec, tab_spec, tab_spec],
        out_specs=out_specs,
        out_shape=out_shape,
        scratch_shapes=[pltpu.VMEM((col_chunk // LANES, tm, LANES), F32)] * min(len(dils), 2),
        compiler_params=_params(("parallel",)),
        name="in_proj",
    )(x2, w_bf, *tables)


def _diffattn_kernel(lam_ref, q_ref, k_ref, vt_ref, g_ref, o_ref, acc_sc,
                     *, tq, tk, unroll, ahead, lam_init):
    qi = pl.program_id(2)
    n_diag = tq // tk
    q = q_ref[0]
    lane = lax.broadcasted_iota(jnp.int32, (tq, LANES), 1)
    zero = jnp.zeros_like(q)
    qs = jnp.concatenate([jnp.where(lane < DA_HEAD_DIM, q, zero),
                          jnp.where(lane >= DA_HEAD_DIM, q, zero)], axis=0)
    acc_sc[...] = jnp.zeros(acc_sc.shape, F32)

    def scores(blk):
        start = pl.multiple_of(blk * tk, tk)
        return lax.dot_general(k_ref[0, pl.ds(start, tk), :], qs, (((1,), (1,)), ((), ())),
                               preferred_element_type=F32)

    def update(s, blk, m_prev, diag):
        if diag is not None:
            krow = diag * tk + lax.broadcasted_iota(jnp.int32, (tk, 2 * tq), 0)
            col = lax.broadcasted_iota(jnp.int32, (tk, 2 * tq), 1)
            s = jnp.where(krow <= jnp.where(col >= tq, col - tq, col), s, NEG_INF)
        m_new = jnp.maximum(m_prev, jnp.max(s, axis=0, keepdims=True))
        alpha = jnp.exp2(m_prev - m_new)
        p = jnp.exp2(s - m_new).astype(BF16)
        pv = jnp.dot(vt_ref[0, 0, blk], p, preferred_element_type=F32)
        acc_sc[...] = alpha * acc_sc[...] + pv
        return m_new

    def run(first_blk, n_plain, m, with_diag):
        n = n_plain + (n_diag if with_diag else 0)
        pending = [scores(first_blk + u) for u in range(min(ahead, n))]
        for u in range(n):
            if u + ahead < n:
                pending.append(scores(first_blk + u + ahead))
            m = update(pending[u], first_blk + u, m, u - n_plain if u >= n_plain else None)
        return m

    n_before = qi * n_diag
    n_groups = n_before // unroll
    m = jnp.full((1, 2 * tq), NEG_INF, F32)
    m = lax.fori_loop(0, n_groups, lambda j, c: run(j * unroll, unroll, c, False), m)
    step = math.gcd(n_diag, unroll)
    tails = [functools.partial(run, n_groups * unroll, r, with_diag=True)
             for r in range(0, unroll, step)]
    lax.switch((n_before - n_groups * unroll) // step, tails, m)

    acc = acc_sc[0:DA_V_DIM, :] / acc_sc[DA_V_DIM:DA_V_DIM + 1, :]
    o = acc[:, :tq] - lam_ref[0] * acc[:, tq:]
    o = o * lax.rsqrt(jnp.mean(o * o, axis=0, keepdims=True) + LN_EPS)
    o_ref[0] = (o.T * g_ref[...] * (1.0 - lam_init)).astype(o_ref.dtype)


def _diff_attention(q, k, vt, lam, sub_g, lam_init, tq=512, unroll=8, ahead=2):
    B, S, _ = q.shape
    tk = DA_TQ
    return pl.pallas_call(
        functools.partial(_diffattn_kernel, tq=tq, tk=tk, unroll=unroll, ahead=ahead,
                          lam_init=lam_init),
        grid=(B, DA_HEADS, S // tq),
        in_specs=[
            pl.BlockSpec(memory_space=pltpu.SMEM),
            pl.BlockSpec((1, tq, LANES), lambda b, h, i: (b, i, h)),
            pl.BlockSpec((1, S, LANES), lambda b, h, i: (b, 0, h)),
            pl.BlockSpec((1, 1, S // tk, DA_VT_ROWS, tk), lambda b, h, i: (b, h, 0, 0, 0)),
            pl.BlockSpec((1, DA_V_DIM), lambda b, h, i: (0, 0)),
        ],
        out_specs=pl.BlockSpec((1, tq, LANES), lambda b, h, i: (b, i, h)),
        out_shape=jax.ShapeDtypeStruct((B, S, DA_WIDTH), BF16),
        scratch_shapes=[pltpu.VMEM((DA_VT_ROWS, 2 * tq), F32)],
        compiler_params=_params(("parallel", "parallel", "arbitrary")),
        name="diff_attn",
    )(lam, q, k, vt, sub_g)


def _hgrn_kernel(g_ref, lb_ref, og_ref, o_ref, state_sc, *, chunks):
    @pl.when(pl.program_id(1) == 0)
    def _():
        state_sc[...] = jnp.zeros(state_sc.shape, F32)

    C, n = HG_CHUNK, chunks
    tc = n * C
    kw, vw = HG_KWIDTH, HG_WIDTH
    hq = g_ref[0, :, 0:kw]
    hf = g_ref[0, :, kw:2 * kw]
    vb = g_ref[0, :, 2 * kw:2 * kw + vw].astype(BF16)
    lb = lb_ref[...]
    q = hq * _sigmoid(hq)
    f = lb + (1.0 - lb) * _sigmoid(hf)
    kk = 1.0 - f
    ri = lax.broadcasted_iota(jnp.int32, (tc, tc), 0)
    ci = lax.broadcasted_iota(jnp.int32, (tc, tc), 1)
    tril = (ci <= ri) & (ci >= ri - ri % C)
    b = jnp.dot(tril.astype(F32), jnp.log(f), preferred_element_type=F32,
                precision=lax.Precision.HIGHEST).reshape(n, C, kw)
    b_last = b[:, C - 1:C, :]
    b_mid = b[:, C // 2 - 1:C // 2, :]
    q3 = q.reshape(n, C, kw)
    k3 = kk.reshape(n, C, kw)
    flat = lambda t: t.reshape(tc, kw).astype(BF16)
    qa = flat(q3 * jnp.exp(b - b_mid))
    ka = flat(k3 * jnp.exp(b_mid - b))
    q_out = flat(q3 * jnp.exp(b))
    k_st = flat(k3 * jnp.exp(b_last - b))
    decay = jnp.exp(b_last)

    outs = []
    for h in range(HG_HEADS):
        kc = slice(h * HG_KDIM, (h + 1) * HG_KDIM)
        vc = slice(h * HG_VDIM, (h + 1) * HG_VDIM)
        a = lax.dot_general(qa[:, kc], ka[:, kc], (((1,), (1,)), ((), ())),
                            preferred_element_type=F32)
        a = jnp.where(tril, a, 0.0).astype(BF16)
        outs.append(jnp.dot(a, vb[:, vc], preferred_element_type=F32))
    states = [state_sc[h] for h in range(HG_HEADS)]
    inter = [[None] * n for _ in range(HG_HEADS)]
    for c in range(n):
        rows = slice(c * C, (c + 1) * C)
        for h in range(HG_HEADS):
            kc = slice(h * HG_KDIM, (h + 1) * HG_KDIM)
            vc = slice(h * HG_VDIM, (h + 1) * HG_VDIM)
            st = states[h]
            inter[h][c] = lax.dot_general(q_out[rows, kc], st.astype(BF16),
                                          (((1,), (1,)), ((), ())), preferred_element_type=F32)
            upd = lax.dot_general(vb[rows, vc], k_st[rows, kc], (((0,), (0,)), ((), ())),
                                  preferred_element_type=F32)
            states[h] = decay[c, :, kc] * st + upd
    og = og_ref[...]
    for h in range(HG_HEADS):
        vc = slice(h * HG_VDIM, (h + 1) * HG_VDIM)
        state_sc[h] = states[h]
        hg = g_ref[0, :, 2 * kw + vw + h * HG_VDIM:2 * kw + vw + (h + 1) * HG_VDIM]
        o = outs[h] + jnp.concatenate(inter[h], axis=0)
        o = _rms_norm(o, og) * (hg * _sigmoid(hg))
        o_ref[0, :, vc] = o.astype(o_ref.dtype)


def _hgrn2(g, lb, out_g, tc=256):
    B, S, W = g.shape
    return pl.pallas_call(
        functools.partial(_hgrn_kernel, chunks=tc // HG_CHUNK),
        grid=(B, S // tc),
        in_specs=[pl.BlockSpec((1, tc, W), lambda b, i: (b, i, 0)),
                  pl.BlockSpec((1, HG_KWIDTH), lambda b, i: (0, 0)),
                  pl.BlockSpec((1, HG_VDIM), lambda b, i: (0, 0))],
        out_specs=pl.BlockSpec((1, tc, HG_WIDTH), lambda b, i: (b, i, 0)),
        out_shape=jax.ShapeDtypeStruct((B, S, HG_WIDTH), BF16),
        scratch_shapes=[pltpu.VMEM((HG_HEADS, HG_VDIM, HG_KDIM), F32)],
        compiler_params=_params(("parallel", "arbitrary")),
        name="hgrn2",
    )(g, lb, out_g)


def _dil_kernel(*refs, tl, n_sub, ahead, first, last):
    q_ref, kp_ref, kc_ref, vp_ref, vc_ref = refs[:5]
    refs = refs[5:]
    if not first:
        accp_ref, mlp_ref = refs[:2]
        refs = refs[2:]
    if last:
        o_ref = refs[0]
        refs = refs[1:]
    else:
        acc_ref, ml_ref = refs[:2]
        refs = refs[2:]
    i = pl.program_id(2)
    n_pairs = DL_HEADS // 2
    half = DL_HEAD_DIM

    if not first:
        acc_sc, ml_sc = refs
        quarter = n_sub * tl // 4
        for c in range(4):
            for hp in range(n_pairs):
                acc_sc[hp, pl.ds(c, quarter, stride=4), :] = (
                    accp_ref[0, c, 0, :, hp * LANES:(hp + 1) * LANES])
            ml_sc[pl.ds(c, quarter, stride=4), :] = mlp_ref[0, c, 0]

    lane = lax.broadcasted_iota(jnp.int32, (tl, LANES), 1)
    low = lane < half

    def pair_rows(x):
        return jnp.concatenate([jnp.broadcast_to(x[:, :tl], (half, tl)),
                                jnp.broadcast_to(x[:, tl:], (half, tl))], axis=0)

    def window(prev_ref, cur_ref, n, cols):
        if isinstance(n, int) and n == 0:
            return jnp.concatenate([prev_ref[0, 0, :, cols], cur_ref[0, 0, 0:tl, cols]], axis=0)
        return cur_ref[0, 0, pl.ds(pl.multiple_of((n - 1) * tl, tl), 2 * tl), cols]

    def rows_of(n):
        return pl.ds(pl.multiple_of(n * tl, tl), tl)

    def scores(n, hp):
        cols = slice(hp * LANES, (hp + 1) * LANES)
        qp = q_ref[0, 0, rows_of(n), cols]
        zero = jnp.zeros_like(qp)
        qs = jnp.concatenate([jnp.where(low, qp, zero), jnp.where(low, zero, qp)], axis=0)
        return lax.dot_general(window(kp_ref, kc_ref, n, cols), qs, (((1,), (1,)), ((), ())),
                               preferred_element_type=F32)

    def update(s, n, hp, ml_t):
        cols = slice(hp * LANES, (hp + 1) * LANES)
        key = lax.broadcasted_iota(jnp.int32, (2 * tl, 2 * tl), 0)
        qcol = lax.broadcasted_iota(jnp.int32, (2 * tl, 2 * tl), 1)
        dist = tl + jnp.where(qcol >= tl, qcol - tl, qcol) - key
        valid = (dist >= 0) & (dist <= DL_SPAN)
        if isinstance(n, int) and n == 0:
            valid = valid & ((key >= tl) | (i > 0))
        s = jnp.where(valid, s, NEG_INF)
        m_new = jnp.max(s, axis=0, keepdims=True)
        if not first:
            h0 = 2 * hp
            m_prev = jnp.concatenate([ml_t[h0:h0 + 1], ml_t[h0 + 1:h0 + 2]], axis=1)
            l_prev = jnp.concatenate([ml_t[DL_HEADS + h0:DL_HEADS + h0 + 1],
                                      ml_t[DL_HEADS + h0 + 1:DL_HEADS + h0 + 2]], axis=1)
            m_new = jnp.maximum(m_prev, m_new)
            alpha = jnp.exp2(m_prev - m_new)
        p = jnp.exp2(s - m_new)
        l_new = jnp.sum(p, axis=0, keepdims=True)
        pv = lax.dot_general(window(vp_ref, vc_ref, n, cols), p.astype(BF16),
                             (((0,), (0,)), ((), ())), preferred_element_type=F32)
        o_t = jnp.concatenate([pv[:half, :tl], pv[half:, tl:]], axis=0)
        if not first:
            l_new = l_new + alpha * l_prev
            o_t = o_t + pair_rows(alpha) * acc_sc[hp, rows_of(n), :].T
        if last:
            o_ref[0, 0, rows_of(n), cols] = (o_t * pair_rows(1.0 / l_new)).T.astype(o_ref.dtype)
        else:
            acc_ref[0, 0, rows_of(n), cols] = o_t.T
        return m_new, l_new

    def sub_block(n):
        ml_t = None if first else ml_sc[rows_of(n), :].T
        ms, ls = [], []
        pending = [scores(n, hp) for hp in range(ahead)]
        for hp in range(n_pairs):
            if hp + ahead < n_pairs:
                pending.append(scores(n, hp + ahead))
            m_new, l_new = update(pending[hp], n, hp, ml_t)
            ms += [m_new[:, :tl], m_new[:, tl:]]
            ls += [l_new[:, :tl], l_new[:, tl:]]
        if not last:
            pad = jnp.zeros((LANES - 2 * DL_HEADS, tl), F32)
            ml_ref[0, 0, rows_of(n), :] = jnp.concatenate(ms + ls + [pad], axis=0).T

    sub_block(0)
    if n_sub > 1:
        def body(n, carry):
            sub_block(n)
            return carry
        lax.fori_loop(1, n_sub, body, 0)


def _dilated_stage(q, k, v, dil, state, last, tl=DL_SPAN, max_sub=4, ahead=4):
    B, _, L, D = q.shape
    first = state is None
    n_sub = min(max_sub, L // tl)
    TL = n_sub * tl
    cur = lambda b, r, i: (b, r, i, 0)
    prev = lambda b, r, i: (b, r, jnp.maximum(i * n_sub - 1, 0), 0)
    tile = pl.BlockSpec((1, 1, TL, D), cur)
    halo = pl.BlockSpec((1, 1, tl, D), prev)
    in_specs = [tile, halo, tile, halo, tile]
    args = [q, k, k, v, v]
    scratch = []
    if not first:
        fine = lambda b, r, i: (b, 0, r, i, 0)
        in_specs += [pl.BlockSpec((1, 4, 1, TL // 4, D), fine),
                     pl.BlockSpec((1, 4, 1, TL // 4, LANES), fine)]
        args += [state[0].reshape(B, 4, dil, L // 4, D), state[1].reshape(B, 4, dil, L // 4, LANES)]
        scratch = [pltpu.VMEM((D // LANES, TL, LANES), F32), pltpu.VMEM((TL, LANES), F32)]
    if last:
        out_shape = [jax.ShapeDtypeStruct((B, dil, L, D), BF16)]
        out_specs = [tile]
    else:
        out_shape = [jax.ShapeDtypeStruct((B, dil, L, D), F32),
                     jax.ShapeDtypeStruct((B, dil, L, LANES), F32)]
        out_specs = [tile, pl.BlockSpec((1, 1, TL, LANES), cur)]
    outs = pl.pallas_call(
        functools.partial(_dil_kernel, tl=tl, n_sub=n_sub, ahead=ahead, first=first, last=last),
        grid=(B, dil, L // TL),
        in_specs=in_specs,
        out_specs=out_specs,
        out_shape=out_shape,
        scratch_shapes=scratch,
        compiler_params=_params(("parallel", "parallel", "arbitrary")),
        name=f"dilated_attn_d{dil}",
    )(*args)
    return outs[0] if last else tuple(outs)


def _dilated_attention(qkv_by_dil):
    dils = sorted((d for _, d in DL_PAIRS), reverse=True)
    assert all(w // d == DL_SPAN for w, d in DL_PAIRS)
    assert all(a == 4 * b for a, b in zip(dils, dils[1:])) and dils[-1] == 1
    state = None
    for d in dils:
        state = _dilated_stage(*qkv_by_dil[d], d, state, last=(d == 1))
    return state


def _tail_kernel(*refs, n_mix, ffn_chunk):
    x_ref = refs[0]
    mix_refs = refs[1:1 + n_mix]
    wo_refs = refs[1 + n_mix:1 + 2 * n_mix]
    (g1_ref, b1_ref, w1_ref, w2_ref, g2_ref, b2_ref,
     p_ref, wp_ref, wg_ref, pg_ref, o_ref) = refs[1 + 2 * n_mix:]
    mix = jnp.dot(mix_refs[0][...], wo_refs[0][...], preferred_element_type=F32)
    for m_ref, w_ref in zip(mix_refs[1:], wo_refs[1:]):
        mix = mix + jnp.dot(m_ref[...], w_ref[...], preferred_element_type=F32)
    x1 = _layer_norm(ALPHA * x_ref[...] + mix, g1_ref[...], b1_ref[...])
    x1b = x1.astype(BF16)
    ffn = None
    for c in range(0, FFN_DIM, ffn_chunk):
        hdn = jnp.dot(x1b, w1_ref[:, c:c + ffn_chunk], preferred_element_type=F32)
        hdn = jnp.square(jnp.maximum(hdn, 0.0)).astype(BF16)
        part = jnp.dot(hdn, w2_ref[c:c + ffn_chunk, :], preferred_element_type=F32)
        ffn = part if ffn is None else ffn + part
    x2 = _layer_norm(ALPHA * x1 + ffn, g2_ref[...], b2_ref[...])
    e = _rms_norm(jnp.dot(p_ref[...].astype(BF16), wp_ref[...], preferred_element_type=F32),
                  pg_ref[...])
    gate = _sigmoid(jnp.dot(x2.astype(BF16), wg_ref[...], preferred_element_type=F32))
    o_ref[...] = x2 + gate * e


def _layer_tail(x2, mixes, w_outs, g1, b1, w1, w2, g2, b2, p2, wp, wg, pg, tm=512,
                ffn_chunk=1024):
    T, D = x2.shape
    rows = lambda w: pl.BlockSpec((tm, w), lambda i: (i, 0))
    vec = lambda a: a.reshape(1, -1)
    in_specs = ([rows(D)] + [rows(m.shape[1]) for m in mixes]
                + [_resident(w.shape) for w in w_outs]
                + [_resident((1, D)), _resident((1, D)), _resident(w1.shape),
                   _resident(w2.shape), _resident((1, D)), _resident((1, D)),
                   rows(PLE_DIM), _resident(wp.shape), _resident(wg.shape),
                   _resident((1, D))])
    return pl.pallas_call(
        functools.partial(_tail_kernel, n_mix=len(mixes), ffn_chunk=ffn_chunk),
        grid=(T // tm,),
        in_specs=in_specs,
        out_specs=rows(D),
        out_shape=jax.ShapeDtypeStruct((T, D), F32),
        compiler_params=_params(("parallel",)),
        name="layer_tail",
    )(x2, *mixes, *w_outs, vec(g1), vec(b1), w1, w2, vec(g2), vec(b2), p2, wp, wg, vec(pg))


def kernel(x, p, ev_w_in, ev_w_out, da_lambda, da_subln_g, hg_lb_logits, hg_norm_g,
           od_w_in, od_w_out, ln1_g, ln1_b, ffn_w1, ffn_w2, ln2_g, ln2_b,
           ple_w_proj, ple_w_gate, ple_norm_g):
    B, S, D = x.shape
    T = B * S
    tables = _rope_lane_tables(S)
    lb_all = jnp.cumsum(jax.nn.softmax(hg_lb_logits.astype(F32), axis=0), axis=0)
    bf = lambda w: w.astype(BF16)
    x2 = x.reshape(T, D)
    for l in range(DEPTH):
        j = l // 2
        if l % 2 == 0:
            lam_init = 0.8 - 0.6 * math.exp(-0.3 * l)
            lp = da_lambda[j].astype(F32)
            lam = (jnp.exp(jnp.sum(lp[0] * lp[1])) - jnp.exp(jnp.sum(lp[2] * lp[3]))
                   + lam_init).reshape(1)
            segs = (("q_log2", DA_QK_WIDTH), ("k", DA_QK_WIDTH), ("vt", DA_WIDTH),
                    ("plain", 2 * HG_KWIDTH + 2 * HG_WIDTH))
            starts = [sum(w for _, w in segs[:n]) for n in range(len(segs))]
            segs = tuple((s, w, kind) for s, (kind, w) in zip(starts, segs))
            q, k, vt, g = _project(x2, bf(ev_w_in[j]), tables, S, segs, (BF16, BF16, BF16, F32))
            o_a = _diff_attention(q.reshape(B, S, -1), k.reshape(B, S, -1), vt,
                                  lam, da_subln_g[j].reshape(1, -1), lam_init)
            o_b = _hgrn2(g.reshape(B, S, -1), lb_all[j].reshape(1, -1),
                         hg_norm_g[j].reshape(1, -1))
            mixes = [o_a.reshape(T, -1), o_b.reshape(T, -1)]
            w_o = bf(ev_w_out[j])
            w_outs = [w_o[:DA_WIDTH], w_o[DA_WIDTH:]]
        else:
            segs = ((0, D, "q_log2"), (D, D, "k"), (2 * D, D, "plain"))
            coarse = tuple(sorted(d for _, d in DL_PAIRS if d > 1))
            outs = _project(x2, bf(od_w_in[j]), tables, S, segs, (BF16, BF16, BF16), dils=coarse)
            per = 1 + len(coarse)
            qkv = {1: tuple(outs[n * per].reshape(B, 1, S, D) for n in range(3))}
            for m, d in enumerate(coarse):
                qkv[d] = tuple(outs[n * per + 1 + m] for n in range(3))
            o = _dilated_attention(qkv)
            mixes = [o.reshape(T, D)]
            w_outs = [bf(od_w_out[j])]
        x2 = _layer_tail(x2, mixes, w_outs, ln1_g[l], ln1_b[l], bf(ffn_w1[l]), bf(ffn_w2[l]),
                         ln2_g[l], ln2_b[l], p[l].reshape(T, PLE_DIM), bf(ple_w_proj[l]),
                         bf(ple_w_gate[l]), ple_norm_g[l])
    return x2.reshape(B, S, D)
```

```python
import functools
import math

import jax
import jax.numpy as jnp
from jax import lax
from jax.experimental import pallas as pl
from jax.experimental.pallas import tpu as pltpu

D_MODEL = 1024
DEPTH = 2
PLE_DIM = 256

DA_HEADS = 4
DA_HEAD_DIM = 64
DA_V_DIM = 2 * DA_HEAD_DIM
DA_QK_WIDTH = DA_HEADS * 2 * DA_HEAD_DIM
DA_WIDTH = DA_HEADS * DA_V_DIM
DA_TQ = 256
DA_VT_ROWS = DA_V_DIM + 16
LOG2_E = math.log2(math.e)

HG_HEADS = 4
HG_KDIM = 128
HG_VDIM = 128
HG_KWIDTH = HG_HEADS * HG_KDIM
HG_WIDTH = HG_HEADS * HG_VDIM
HG_CHUNK = 64

DL_HEADS = 16
DL_HEAD_DIM = D_MODEL // DL_HEADS
DL_PAIRS = ((128, 1), (512, 4), (2048, 16))
DL_SPAN = 128

FFN_DIM = 4 * D_MODEL
ROPE_THETA = 500000.0
ROT_DIM = 16
ALPHA = (2 * DEPTH) ** 0.25
LN_EPS = 1e-5

LANES = 128
VMEM_LIMIT_BYTES = 56 * 1024 * 1024

F32 = jnp.float32
BF16 = jnp.bfloat16
NEG_INF = float("-inf")


def _params(sem, vmem=VMEM_LIMIT_BYTES):
    return pltpu.CompilerParams(dimension_semantics=sem, vmem_limit_bytes=vmem)


def _resident(shape):
    nd = len(shape)
    return pl.BlockSpec(shape, lambda *_: (0,) * nd, pipeline_mode=pl.Buffered(1))


def _sigmoid(x):
    return 1.0 / (1.0 + jnp.exp(-x))


def _layer_norm(x, g, b):
    mu = jnp.mean(x, axis=-1, keepdims=True)
    xc = x - mu
    var = jnp.mean(xc * xc, axis=-1, keepdims=True)
    return xc * lax.rsqrt(var + LN_EPS) * g + b


def _rms_norm(x, g):
    return x * lax.rsqrt(jnp.mean(x * x, axis=-1, keepdims=True) + LN_EPS) * g


def _rope_lane_tables(seq):
    half = ROT_DIM // 2
    inv = ROPE_THETA ** (-jnp.arange(0, ROT_DIM, 2, dtype=F32) / ROT_DIM)
    lane = jnp.arange(LANES) % 64
    inv_lane = jnp.where(lane < ROT_DIM, inv[lane % half], 0.0)
    ang = jnp.arange(seq, dtype=F32)[:, None] * inv_lane[None, :]
    cos, sin = jnp.cos(ang), jnp.sin(ang)
    s1 = jnp.where((lane >= half) & (lane < ROT_DIM), sin, 0.0)
    s2 = jnp.where(lane < half, -sin, 0.0)
    return cos, s1, s2


def _proj_kernel(x_ref, w_ref, c_ref, s1_ref, s2_ref, *refs, segs, dils, col_chunk, tm):
    n_out = len(segs) * (1 + len(dils))
    out_refs, stage = refs[:n_out], refs[n_out:]
    xb = x_ref[...].astype(BF16)
    c, s1, s2 = c_ref[...], s1_ref[...], s2_ref[...]
    for n, (start, width, kind) in enumerate(segs):
        nat_ref = out_refs[n * (1 + len(dils))]
        dil_refs = out_refs[n * (1 + len(dils)) + 1:(n + 1) * (1 + len(dils))]
        for j in range(0, width, col_chunk):
            acc = jnp.dot(xb, w_ref[:, start + j:start + j + col_chunk],
                          preferred_element_type=F32)
            if kind == "vt":
                for g in range(col_chunk // LANES):
                    head = (j + g * LANES) // DA_V_DIM
                    for sb in range(tm // DA_TQ):
                        t = acc[sb * DA_TQ:(sb + 1) * DA_TQ, g * LANES:(g + 1) * LANES].T
                        nat_ref[0, head, sb, 0:DA_V_DIM, :] = t.astype(nat_ref.dtype)
                        nat_ref[0, head, sb, DA_V_DIM:, :] = jnp.ones(
                            (DA_VT_ROWS - DA_V_DIM, DA_TQ), nat_ref.dtype)
                continue
            if kind != "plain":
                scale = {"q_log2": 64 ** -0.5 * LOG2_E, "k": None}[kind]
                parts = []
                for g in range(col_chunk // LANES):
                    a = acc[:, g * LANES:(g + 1) * LANES]
                    r = (a * c + pltpu.roll(a, ROT_DIM // 2, 1) * s1
                         + pltpu.roll(a, LANES - ROT_DIM // 2, 1) * s2)
                    parts.append(r if scale is None else r * scale)
                acc = jnp.concatenate(parts, axis=1)
            nat_ref[:, j:j + col_chunk] = acc.astype(nat_ref.dtype)
            if dils:
                for g in range(col_chunk // LANES):
                    cols = slice(j + g * LANES, j + (g + 1) * LANES)
                    src, d_prev = stage[0], 1
                    src[g] = acc[:, g * LANES:(g + 1) * LANES]
                    for n_d, (d, ref) in enumerate(zip(dils, dil_refs)):
                        ratio, rows = d // d_prev, tm // d
                        dst = stage[(n_d + 1) % 2] if n_d + 1 < len(dils) else None
                        for r_prev in range(d_prev):
                            for c_new in range(ratio):
                                r = r_prev + d_prev * c_new
                                piece = src[g, pl.ds(r_prev * (tm // d_prev) + c_new, rows,
                                                     stride=ratio), :]
                                ref[0, r, :, cols] = piece.astype(ref.dtype)
                                if dst is not None:
                                    dst[g, r * rows:(r + 1) * rows, :] = piece
                        src, d_prev = dst, d


def _project(x2, w_bf, tables, seq, segs, out_dtypes, dils=(), tm=512, col_chunk=512):
    T, D = x2.shape
    N = w_bf.shape[1]
    nsb = seq // tm
    assert all(b % a == 0 for a, b in zip((1,) + tuple(dils), dils))
    tab_spec = pl.BlockSpec((tm, LANES), lambda i: (i % nsb, 0))
    out_shape, out_specs = [], []
    for (_, w, kind), dt in zip(segs, out_dtypes):
        if kind == "vt":
            heads, blocks = w // DA_V_DIM, tm // DA_TQ
            out_shape.append(jax.ShapeDtypeStruct(
                (T // seq, heads, seq // DA_TQ, DA_VT_ROWS, DA_TQ), dt))
            out_specs.append(pl.BlockSpec((1, heads, blocks, DA_VT_ROWS, DA_TQ),
                                          lambda i: (i // nsb, 0, i % nsb, 0, 0)))
            continue
        out_shape.append(jax.ShapeDtypeStruct((T, w), dt))
        out_specs.append(pl.BlockSpec((tm, w), lambda i: (i, 0)))
        for d in dils:
            out_shape.append(jax.ShapeDtypeStruct((T // seq, d, seq // d, w), dt))
            out_specs.append(pl.BlockSpec((1, d, tm // d, w), lambda i: (i // nsb, 0, i % nsb, 0)))
    return pl.pallas_call(
        functools.partial(_proj_kernel, segs=segs, dils=dils, col_chunk=col_chunk, tm=tm),
        grid=(T // tm,),
        in_specs=[pl.BlockSpec((tm, D), lambda i: (i, 0)), _resident((D, N)),
                  tab_spec, tab_spec, tab_spec],
        out_specs=out_specs,
        out_shape=out_shape,
        scratch_shapes=[pltpu.VMEM((col_chunk // LANES, tm, LANES), F32)] * min(len(dils), 2),
        compiler_params=_params(("parallel",)),
        name="in_proj",
    )(x2, w_bf, *tables)


def _diffattn_kernel(lam_ref, q_ref, k_ref, vt_ref, g_ref, o_ref, acc_sc,
                     *, tq, tk, unroll, ahead, lam_init):
    qi = pl.program_id(2)
    n_diag = tq // tk
    q = q_ref[0]
    lane = lax.broadcasted_iota(jnp.int32, (tq, LANES), 1)
    zero = jnp.zeros_like(q)
    qs = jnp.concatenate([jnp.where(lane < DA_HEAD_DIM, q, zero),
                          jnp.where(lane >= DA_HEAD_DIM, q, zero)], axis=0)
    acc_sc[...] = jnp.zeros(acc_sc.shape, F32)

    def scores(blk):
        start = pl.multiple_of(blk * tk, tk)
        return lax.dot_general(k_ref[0, pl.ds(start, tk), :], qs, (((1,), (1,)), ((), ())),
                               preferred_element_type=F32)

    def update(s, blk, m_prev, diag):
        if diag is not None:
            krow = diag * tk + lax.broadcasted_iota(jnp.int32, (tk, 2 * tq), 0)
            col = lax.broadcasted_iota(jnp.int32, (tk, 2 * tq), 1)
            s = jnp.where(krow <= jnp.where(col >= tq, col - tq, col), s, NEG_INF)
        m_new = jnp.maximum(m_prev, jnp.max(s, axis=0, keepdims=True))
        alpha = jnp.exp2(m_prev - m_new)
        p = jnp.exp2(s - m_new).astype(BF16)
        pv = jnp.dot(vt_ref[0, 0, blk], p, preferred_element_type=F32)
        acc_sc[...] = alpha * acc_sc[...] + pv
        return m_new

    def run(first_blk, n_plain, m, with_diag):
        n = n_plain + (n_diag if with_diag else 0)
        pending = [scores(first_blk + u) for u in range(min(ahead, n))]
        for u in range(n):
            if u + ahead < n:
                pending.append(scores(first_blk + u + ahead))
            m = update(pending[u], first_blk + u, m, u - n_plain if u >= n_plain else None)
        return m

    n_before = qi * n_diag
    n_groups = n_before // unroll
    m = jnp.full((1, 2 * tq), NEG_INF, F32)
    m = lax.fori_loop(0, n_groups, lambda j, c: run(j * unroll, unroll, c, False), m)
    step = math.gcd(n_diag, unroll)
    tails = [functools.partial(run, n_groups * unroll, r, with_diag=True)
             for r in range(0, unroll, step)]
    lax.switch((n_before - n_groups * unroll) // step, tails, m)

    acc = acc_sc[0:DA_V_DIM, :] / acc_sc[DA_V_DIM:DA_V_DIM + 1, :]
    o = acc[:, :tq] - lam_ref[0] * acc[:, tq:]
    o = o * lax.rsqrt(jnp.mean(o * o, axis=0, keepdims=True) + LN_EPS)
    o_ref[0] = (o.T * g_ref[...] * (1.0 - lam_init)).astype(o_ref.dtype)


def _diff_attention(q, k, vt, lam, sub_g, lam_init, tq=512, unroll=8, ahead=2):
    B, S, _ = q.shape
    tk = DA_TQ
    return pl.pallas_call(
        functools.partial(_diffattn_kernel, tq=tq, tk=tk, unroll=unroll, ahead=ahead,
                          lam_init=lam_init),
        grid=(B, DA_HEADS, S // tq),
        in_specs=[
            pl.BlockSpec(memory_space=pltpu.SMEM),
            pl.BlockSpec((1, tq, LANES), lambda b, h, i: (b, i, h)),
            pl.BlockSpec((1, S, LANES), lambda b, h, i: (b, 0, h)),
            pl.BlockSpec((1, 1, S // tk, DA_VT_ROWS, tk), lambda b, h, i: (b, h, 0, 0, 0)),
            pl.BlockSpec((1, DA_V_DIM), lambda b, h, i: (0, 0)),
        ],
        out_specs=pl.BlockSpec((1, tq, LANES), lambda b, h, i: (b, i, h)),
        out_shape=jax.ShapeDtypeStruct((B, S, DA_WIDTH), BF16),
        scratch_shapes=[pltpu.VMEM((DA_VT_ROWS, 2 * tq), F32)],
        compiler_params=_params(("parallel", "parallel", "arbitrary")),
        name="diff_attn",
    )(lam, q, k, vt, sub_g)


def _hgrn_kernel(g_ref, lb_ref, og_ref, o_ref, state_sc, *, chunks):
    @pl.when(pl.program_id(1) == 0)
    def _():
        state_sc[...] = jnp.zeros(state_sc.shape, F32)

    C, n = HG_CHUNK, chunks
    tc = n * C
    kw, vw = HG_KWIDTH, HG_WIDTH
    hq = g_ref[0, :, 0:kw]
    hf = g_ref[0, :, kw:2 * kw]
    vb = g_ref[0, :, 2 * kw:2 * kw + vw].astype(BF16)
    lb = lb_ref[...]
    q = hq * _sigmoid(hq)
    f = lb + (1.0 - lb) * _sigmoid(hf)
    kk = 1.0 - f
    ri = lax.broadcasted_iota(jnp.int32, (tc, tc), 0)
    ci = lax.broadcasted_iota(jnp.int32, (tc, tc), 1)
    tril = (ci <= ri) & (ci >= ri - ri % C)
    logf = jnp.log(f)
    hi = logf.astype(BF16)
    rest = logf - hi.astype(F32)
    mid = rest.astype(BF16)
    lo = (rest - mid.astype(F32)).astype(BF16)
    parts = jnp.dot(tril.astype(BF16), jnp.concatenate([hi, mid, lo], axis=1),
                    preferred_element_type=F32)
    b = (parts[:, :kw] + parts[:, kw:2 * kw] + parts[:, 2 * kw:]).reshape(n, C, kw)
    b_last = b[:, C - 1:C, :]
    b_mid = b[:, C // 2 - 1:C // 2, :]
    q3 = q.reshape(n, C, kw)
    k3 = kk.reshape(n, C, kw)
    flat = lambda t: t.reshape(tc, kw).astype(BF16)
    qa = flat(q3 * jnp.exp(b - b_mid))
    ka = flat(k3 * jnp.exp(b_mid - b))
    q_out = flat(q3 * jnp.exp(b))
    k_st = flat(k3 * jnp.exp(b_last - b))
    decay = jnp.exp(b_last)

    outs = []
    for h in range(HG_HEADS):
        kc = slice(h * HG_KDIM, (h + 1) * HG_KDIM)
        vc = slice(h * HG_VDIM, (h + 1) * HG_VDIM)
        a = lax.dot_general(qa[:, kc], ka[:, kc], (((1,), (1,)), ((), ())),
                            preferred_element_type=F32)
        a = jnp.where(tril, a, 0.0).astype(BF16)
        outs.append(jnp.dot(a, vb[:, vc], preferred_element_type=F32))
    states = [state_sc[h] for h in range(HG_HEADS)]
    inter = [[None] * n for _ in range(HG_HEADS)]
    for c in range(n):
        rows = slice(c * C, (c + 1) * C)
        for h in range(HG_HEADS):
            kc = slice(h * HG_KDIM, (h + 1) * HG_KDIM)
            vc = slice(h * HG_VDIM, (h + 1) * HG_VDIM)
            st = states[h]
            inter[h][c] = lax.dot_general(q_out[rows, kc], st.astype(BF16),
                                          (((1,), (1,)), ((), ())), preferred_element_type=F32)
            upd = lax.dot_general(vb[rows, vc], k_st[rows, kc], (((0,), (0,)), ((), ())),
                                  preferred_element_type=F32)
            states[h] = decay[c, :, kc] * st + upd
    og = og_ref[...]
    for h in range(HG_HEADS):
        vc = slice(h * HG_VDIM, (h + 1) * HG_VDIM)
        state_sc[h] = states[h]
        hg = g_ref[0, :, 2 * kw + vw + h * HG_VDIM:2 * kw + vw + (h + 1) * HG_VDIM]
        o = outs[h] + jnp.concatenate(inter[h], axis=0)
        o = _rms_norm(o, og) * (hg * _sigmoid(hg))
        o_ref[0, :, vc] = o.astype(o_ref.dtype)


def _hgrn2(g, lb, out_g, tc=256):
    B, S, W = g.shape
    return pl.pallas_call(
        functools.partial(_hgrn_kernel, chunks=tc // HG_CHUNK),
        grid=(B, S // tc),
        in_specs=[pl.BlockSpec((1, tc, W), lambda b, i: (b, i, 0)),
                  pl.BlockSpec((1, HG_KWIDTH), lambda b, i: (0, 0)),
                  pl.BlockSpec((1, HG_VDIM), lambda b, i: (0, 0))],
        out_specs=pl.BlockSpec((1, tc, HG_WIDTH), lambda b, i: (b, i, 0)),
        out_shape=jax.ShapeDtypeStruct((B, S, HG_WIDTH), BF16),
        scratch_shapes=[pltpu.VMEM((HG_HEADS, HG_VDIM, HG_KDIM), F32)],
        compiler_params=_params(("parallel", "arbitrary")),
        name="hgrn2",
    )(g, lb, out_g)


def _dil_kernel(*refs, tl, n_sub, ahead, first, last):
    q_ref, kp_ref, kc_ref, vp_ref, vc_ref = refs[:5]
    refs = refs[5:]
    if not first:
        accp_ref, mlp_ref = refs[:2]
        refs = refs[2:]
    if last:
        o_ref = refs[0]
        refs = refs[1:]
    else:
        acc_ref, ml_ref = refs[:2]
        refs = refs[2:]
    i = pl.program_id(2)
    n_pairs = DL_HEADS // 2
    half = DL_HEAD_DIM

    if not first:
        acc_sc, ml_sc = refs
        quarter = n_sub * tl // 4
        for c in range(4):
            for hp in range(n_pairs):
                acc_sc[hp, pl.ds(c, quarter, stride=4), :] = (
                    accp_ref[0, c, 0, :, hp * LANES:(hp + 1) * LANES])
            ml_sc[pl.ds(c, quarter, stride=4), :] = mlp_ref[0, c, 0]

    lane = lax.broadcasted_iota(jnp.int32, (tl, LANES), 1)
    low = lane < half

    def pair_rows(x):
        return jnp.concatenate([jnp.broadcast_to(x[:, :tl], (half, tl)),
                                jnp.broadcast_to(x[:, tl:], (half, tl))], axis=0)

    def window(prev_ref, cur_ref, n, cols):
        if n == 0:
            return jnp.concatenate([prev_ref[0, 0, :, cols], cur_ref[0, 0, 0:tl, cols]], axis=0)
        return cur_ref[0, 0, (n - 1) * tl:(n + 1) * tl, cols]

    def rows_of(n):
        return slice(n * tl, (n + 1) * tl)

    def scores(n, hp):
        cols = slice(hp * LANES, (hp + 1) * LANES)
        qp = q_ref[0, 0, rows_of(n), cols]
        zero = jnp.zeros_like(qp)
        qs = jnp.concatenate([jnp.where(low, qp, zero), jnp.where(low, zero, qp)], axis=0)
        return lax.dot_general(window(kp_ref, kc_ref, n, cols), qs, (((1,), (1,)), ((), ())),
                               preferred_element_type=F32)

    def update(s, n, hp, ml_t):
        cols = slice(hp * LANES, (hp + 1) * LANES)
        key = lax.broadcasted_iota(jnp.int32, (2 * tl, 2 * tl), 0)
        qcol = lax.broadcasted_iota(jnp.int32, (2 * tl, 2 * tl), 1)
        dist = tl + jnp.where(qcol >= tl, qcol - tl, qcol) - key
        valid = (dist >= 0) & (dist <= DL_SPAN)
        if n == 0:
            valid = valid & ((key >= tl) | (i > 0))
        s = jnp.where(valid, s, NEG_INF)
        m_new = jnp.max(s, axis=0, keepdims=True)
        if not first:
            h0 = 2 * hp
            m_prev = jnp.concatenate([ml_t[h0:h0 + 1], ml_t[h0 + 1:h0 + 2]], axis=1)
            l_prev = jnp.concatenate([ml_t[DL_HEADS + h0:DL_HEADS + h0 + 1],
                                      ml_t[DL_HEADS + h0 + 1:DL_HEADS + h0 + 2]], axis=1)
            m_new = jnp.maximum(m_prev, m_new)
            alpha = jnp.exp2(m_prev - m_new)
        p = jnp.exp2(s - m_new)
        l_new = jnp.sum(p, axis=0, keepdims=True)
        pv = lax.dot_general(window(vp_ref, vc_ref, n, cols), p.astype(BF16),
                             (((0,), (0,)), ((), ())), preferred_element_type=F32)
        o_t = jnp.concatenate([pv[:half, :tl], pv[half:, tl:]], axis=0)
        if not first:
            l_new = l_new + alpha * l_prev
            o_t = o_t + pair_rows(alpha) * acc_sc[hp, rows_of(n), :].T
        if last:
            o_ref[0, 0, rows_of(n), cols] = (o_t * pair_rows(1.0 / l_new)).T.astype(o_ref.dtype)
        else:
            acc_ref[0, 0, rows_of(n), cols] = o_t.T
        return m_new, l_new

    blocks = [(n, hp) for n in range(n_sub) for hp in range(n_pairs)]
    pending = [scores(*blk) for blk in blocks[:ahead]]
    for idx, (n, hp) in enumerate(blocks):
        if idx + ahead < len(blocks):
            pending.append(scores(*blocks[idx + ahead]))
        if hp == 0:
            ml_t = None if first else ml_sc[rows_of(n), :].T
            ms, ls = [], []
        m_new, l_new = update(pending[idx], n, hp, ml_t)
        pending[idx] = None
        ms += [m_new[:, :tl], m_new[:, tl:]]
        ls += [l_new[:, :tl], l_new[:, tl:]]
        if hp == n_pairs - 1 and not last:
            pad = jnp.zeros((LANES - 2 * DL_HEADS, tl), F32)
            ml_ref[0, 0, rows_of(n), :] = jnp.concatenate(ms + ls + [pad], axis=0).T


def _dilated_stage(q, k, v, dil, state, last, tl=DL_SPAN, max_sub=4, ahead=4):
    B, _, L, D = q.shape
    first = state is None
    n_sub = min(max_sub, L // tl)
    TL = n_sub * tl
    cur = lambda b, r, i: (b, r, i, 0)
    prev = lambda b, r, i: (b, r, jnp.maximum(i * n_sub - 1, 0), 0)
    tile = pl.BlockSpec((1, 1, TL, D), cur)
    halo = pl.BlockSpec((1, 1, tl, D), prev)
    in_specs = [tile, halo, tile, halo, tile]
    args = [q, k, k, v, v]
    scratch = []
    if not first:
        fine = lambda b, r, i: (b, 0, r, i, 0)
        in_specs += [pl.BlockSpec((1, 4, 1, TL // 4, D), fine),
                     pl.BlockSpec((1, 4, 1, TL // 4, LANES), fine)]
        args += [state[0].reshape(B, 4, dil, L // 4, D), state[1].reshape(B, 4, dil, L // 4, LANES)]
        scratch = [pltpu.VMEM((D // LANES, TL, LANES), F32), pltpu.VMEM((TL, LANES), F32)]
    if last:
        out_shape = [jax.ShapeDtypeStruct((B, dil, L, D), BF16)]
        out_specs = [tile]
    else:
        out_shape = [jax.ShapeDtypeStruct((B, dil, L, D), F32),
                     jax.ShapeDtypeStruct((B, dil, L, LANES), F32)]
        out_specs = [tile, pl.BlockSpec((1, 1, TL, LANES), cur)]
    outs = pl.pallas_call(
        functools.partial(_dil_kernel, tl=tl, n_sub=n_sub, ahead=ahead, first=first, last=last),
        grid=(B, dil, L // TL),
        in_specs=in_specs,
        out_specs=out_specs,
        out_shape=out_shape,
        scratch_shapes=scratch,
        compiler_params=_params(("parallel", "parallel", "arbitrary")),
        name=f"dilated_attn_d{dil}",
    )(*args)
    return outs[0] if last else tuple(outs)


def _dilated_attention(qkv_by_dil):
    dils = sorted((d for _, d in DL_PAIRS), reverse=True)
    assert all(w // d == DL_SPAN for w, d in DL_PAIRS)
    assert all(a == 4 * b for a, b in zip(dils, dils[1:])) and dils[-1] == 1
    state = None
    for d in dils:
        state = _dilated_stage(*qkv_by_dil[d], d, state, last=(d == 1))
    return state


def _tail_kernel(*refs, n_mix, ffn_chunk):
    x_ref = refs[0]
    mix_refs = refs[1:1 + n_mix]
    wo_refs = refs[1 + n_mix:1 + 2 * n_mix]
    (g1_ref, b1_ref, w1_ref, w2_ref, g2_ref, b2_ref,
     p_ref, wp_ref, wg_ref, pg_ref, o_ref) = refs[1 + 2 * n_mix:]
    mix = jnp.dot(mix_refs[0][...], wo_refs[0][...], preferred_element_type=F32)
    for m_ref, w_ref in zip(mix_refs[1:], wo_refs[1:]):
        mix = mix + jnp.dot(m_ref[...], w_ref[...], preferred_element_type=F32)
    x1 = _layer_norm(ALPHA * x_ref[...] + mix, g1_ref[...], b1_ref[...])
    x1b = x1.astype(BF16)
    ffn = None
    for c in range(0, FFN_DIM, ffn_chunk):
        hdn = jnp.dot(x1b, w1_ref[:, c:c + ffn_chunk], preferred_element_type=F32)
        hdn = jnp.square(jnp.maximum(hdn, 0.0)).astype(BF16)
        part = jnp.dot(hdn, w2_ref[c:c + ffn_chunk, :], preferred_element_type=F32)
        ffn = part if ffn is None else ffn + part
    x2 = _layer_norm(ALPHA * x1 + ffn, g2_ref[...], b2_ref[...])
    e = _rms_norm(jnp.dot(p_ref[...].astype(BF16), wp_ref[...], preferred_element_type=F32),
                  pg_ref[...])
    gate = _sigmoid(jnp.dot(x2.astype(BF16), wg_ref[...], preferred_element_type=F32))
    o_ref[...] = x2 + gate * e


def _layer_tail(x2, mixes, w_outs, g1, b1, w1, w2, g2, b2, p2, wp, wg, pg, tm=512,
                ffn_chunk=1024):
    T, D = x2.shape
    rows = lambda w: pl.BlockSpec((tm, w), lambda i: (i, 0))
    vec = lambda a: a.reshape(1, -1)
    in_specs = ([rows(D)] + [rows(m.shape[1]) for m in mixes]
                + [_resident(w.shape) for w in w_outs]
                + [_resident((1, D)), _resident((1, D)), _resident(w1.shape),
                   _resident(w2.shape), _resident((1, D)), _resident((1, D)),
                   rows(PLE_DIM), _resident(wp.shape), _resident(wg.shape),
                   _resident((1, D))])
    return pl.pallas_call(
        functools.partial(_tail_kernel, n_mix=len(mixes), ffn_chunk=ffn_chunk),
        grid=(T // tm,),
        in_specs=in_specs,
        out_specs=rows(D),
        out_shape=jax.ShapeDtypeStruct((T, D), F32),
        compiler_params=_params(("parallel",)),
        name="layer_tail",
    )(x2, *mixes, *w_outs, vec(g1), vec(b1), w1, w2, vec(g2), vec(b2), p2, wp, wg, vec(pg))


def kernel(x, p, ev_w_in, ev_w_out, da_lambda, da_subln_g, hg_lb_logits, hg_norm_g,
           od_w_in, od_w_out, ln1_g, ln1_b, ffn_w1, ffn_w2, ln2_g, ln2_b,
           ple_w_proj, ple_w_gate, ple_norm_g):
    B, S, D = x.shape
    T = B * S
    tables = _rope_lane_tables(S)
    lb_all = jnp.cumsum(jax.nn.softmax(hg_lb_logits.astype(F32), axis=0), axis=0)
    bf = lambda w: w.astype(BF16)
    x2 = x.reshape(T, D)
    for l in range(DEPTH):
        j = l // 2
        if l % 2 == 0:
            lam_init = 0.8 - 0.6 * math.exp(-0.3 * l)
            lp = da_lambda[j].astype(F32)
            lam = (jnp.exp(jnp.sum(lp[0] * lp[1])) - jnp.exp(jnp.sum(lp[2] * lp[3]))
                   + lam_init).reshape(1)
            segs = (("q_log2", DA_QK_WIDTH), ("k", DA_QK_WIDTH), ("vt", DA_WIDTH),
                    ("plain", 2 * HG_KWIDTH + 2 * HG_WIDTH))
            starts = [sum(w for _, w in segs[:n]) for n in range(len(segs))]
            segs = tuple((s, w, kind) for s, (kind, w) in zip(starts, segs))
            q, k, vt, g = _project(x2, bf(ev_w_in[j]), tables, S, segs, (BF16, BF16, BF16, F32))
            o_a = _diff_attention(q.reshape(B, S, -1), k.reshape(B, S, -1), vt,
                                  lam, da_subln_g[j].reshape(1, -1), lam_init)
            o_b = _hgrn2(g.reshape(B, S, -1), lb_all[j].reshape(1, -1),
                         hg_norm_g[j].reshape(1, -1))
            mixes = [o_a.reshape(T, -1), o_b.reshape(T, -1)]
            w_o = bf(ev_w_out[j])
            w_outs = [w_o[:DA_WIDTH], w_o[DA_WIDTH:]]
        else:
            segs = ((0, D, "q_log2"), (D, D, "k"), (2 * D, D, "plain"))
            coarse = tuple(sorted(d for _, d in DL_PAIRS if d > 1))
            outs = _project(x2, bf(od_w_in[j]), tables, S, segs, (BF16, BF16, BF16), dils=coarse)
            per = 1 + len(coarse)
            qkv = {1: tuple(outs[n * per].reshape(B, 1, S, D) for n in range(3))}
            for m, d in enumerate(coarse):
                qkv[d] = tuple(outs[n * per + 1 + m] for n in range(3))
            o = _dilated_attention(qkv)
            mixes = [o.reshape(T, D)]
            w_outs = [bf(od_w_out[j])]
        x2 = _layer_tail(x2, mixes, w_outs, ln1_g[l], ln1_b[l], bf(ffn_w1[l]), bf(ffn_w2[l]),
                         ln2_g[l], ln2_b[l], p[l].reshape(T, PLE_DIM), bf(ple_w_proj[l]),
                         bf(ple_w_gate[l]), ple_norm_g[l])
    return x2.reshape(B, S, D)
```

```python
import functools
import math

import jax
import jax.numpy as jnp
from jax import lax
from jax.experimental import pallas as pl
from jax.experimental.pallas import tpu as pltpu

D_MODEL = 1024
DEPTH = 2
PLE_DIM = 256

DA_HEADS = 4
DA_HEAD_DIM = 64
DA_V_DIM = 2 * DA_HEAD_DIM
DA_QK_WIDTH = DA_HEADS * 2 * DA_HEAD_DIM
DA_WIDTH = DA_HEADS * DA_V_DIM
DA_TQ = 256
DA_VT_ROWS = DA_V_DIM + 16
LOG2_E = math.log2(math.e)

HG_HEADS = 4
HG_KDIM = 128
HG_VDIM = 128
HG_KWIDTH = HG_HEADS * HG_KDIM
HG_WIDTH = HG_HEADS * HG_VDIM
HG_CHUNK = 64

DL_HEADS = 16
DL_HEAD_DIM = D_MODEL // DL_HEADS
DL_PAIRS = ((128, 1), (512, 4), (2048, 16))
DL_SPAN = 128

FFN_DIM = 4 * D_MODEL
ROPE_THETA = 500000.0
ROT_DIM = 16
ALPHA = (2 * DEPTH) ** 0.25
LN_EPS = 1e-5

LANES = 128
VMEM_LIMIT_BYTES = 56 * 1024 * 1024

F32 = jnp.float32
BF16 = jnp.bfloat16
NEG_INF = float("-inf")


def _params(sem, vmem=VMEM_LIMIT_BYTES):
    return pltpu.CompilerParams(dimension_semantics=sem, vmem_limit_bytes=vmem)


def _resident(shape):
    nd = len(shape)
    return pl.BlockSpec(shape, lambda *_: (0,) * nd, pipeline_mode=pl.Buffered(1))


def _sigmoid(x):
    return 1.0 / (1.0 + jnp.exp(-x))


def _layer_norm(x, g, b):
    mu = jnp.mean(x, axis=-1, keepdims=True)
    xc = x - mu
    var = jnp.mean(xc * xc, axis=-1, keepdims=True)
    return xc * lax.rsqrt(var + LN_EPS) * g + b


def _rms_norm(x, g):
    return x * lax.rsqrt(jnp.mean(x * x, axis=-1, keepdims=True) + LN_EPS) * g


def _rope_lane_tables(seq):
    half = ROT_DIM // 2
    inv = ROPE_THETA ** (-jnp.arange(0, ROT_DIM, 2, dtype=F32) / ROT_DIM)
    lane = jnp.arange(LANES) % 64
    inv_lane = jnp.where(lane < ROT_DIM, inv[lane % half], 0.0)
    ang = jnp.arange(seq, dtype=F32)[:, None] * inv_lane[None, :]
    cos, sin = jnp.cos(ang), jnp.sin(ang)
    s1 = jnp.where((lane >= half) & (lane < ROT_DIM), sin, 0.0)
    s2 = jnp.where(lane < half, -sin, 0.0)
    return cos, s1, s2


def _rope(acc, c, s1, s2, scale):
    parts = []
    for g in range(acc.shape[1] // LANES):
        a = acc[:, g * LANES:(g + 1) * LANES]
        r = (a * c + pltpu.roll(a, ROT_DIM // 2, 1) * s1
             + pltpu.roll(a, LANES - ROT_DIM // 2, 1) * s2)
        parts.append(r if scale is None else r * scale)
    return jnp.concatenate(parts, axis=1)


Q_SCALE = 64 ** -0.5 * LOG2_E


def _proj_kernel(x_ref, w_ref, c_ref, s1_ref, s2_ref, *refs, segs, dils, col_chunk, tm):
    n_out = len(segs) * (1 + len(dils))
    out_refs, stage = refs[:n_out], refs[n_out:]
    xb = x_ref[...].astype(BF16)
    c, s1, s2 = c_ref[...], s1_ref[...], s2_ref[...]
    for n, (start, width, kind) in enumerate(segs):
        nat_ref = out_refs[n * (1 + len(dils))]
        dil_refs = out_refs[n * (1 + len(dils)) + 1:(n + 1) * (1 + len(dils))]
        for j in range(0, width, col_chunk):
            acc = jnp.dot(xb, w_ref[:, start + j:start + j + col_chunk],
                          preferred_element_type=F32)
            if kind != "plain":
                acc = _rope(acc, c, s1, s2, {"q": Q_SCALE, "k": None}[kind])
            nat_ref[:, j:j + col_chunk] = acc.astype(nat_ref.dtype)
            for g in range(col_chunk // LANES):
                cols = slice(j + g * LANES, j + (g + 1) * LANES)
                src, d_prev = stage[0], 1
                src[g] = acc[:, g * LANES:(g + 1) * LANES]
                for n_d, (d, ref) in enumerate(zip(dils, dil_refs)):
                    ratio, rows = d // d_prev, tm // d
                    dst = stage[(n_d + 1) % 2] if n_d + 1 < len(dils) else None
                    for r_prev in range(d_prev):
                        for c_new in range(ratio):
                            r = r_prev + d_prev * c_new
                            piece = src[g, pl.ds(r_prev * (tm // d_prev) + c_new, rows,
                                                 stride=ratio), :]
                            ref[0, r, :, cols] = piece.astype(ref.dtype)
                            if dst is not None:
                                dst[g, r * rows:(r + 1) * rows, :] = piece
                    src, d_prev = dst, d


def _project(x2, w_bf, tables, seq, segs, dils, tm=512, col_chunk=512):
    T, D = x2.shape
    N = w_bf.shape[1]
    nsb = seq // tm
    assert dils and all(b % a == 0 for a, b in zip((1,) + tuple(dils), dils))
    tab_spec = pl.BlockSpec((tm, LANES), lambda i: (i % nsb, 0))
    out_shape, out_specs = [], []
    for (_, w, _) in segs:
        out_shape.append(jax.ShapeDtypeStruct((T, w), BF16))
        out_specs.append(pl.BlockSpec((tm, w), lambda i: (i, 0)))
        for d in dils:
            out_shape.append(jax.ShapeDtypeStruct((T // seq, d, seq // d, w), BF16))
            out_specs.append(pl.BlockSpec((1, d, tm // d, w), lambda i: (i // nsb, 0, i % nsb, 0)))
    return pl.pallas_call(
        functools.partial(_proj_kernel, segs=segs, dils=dils, col_chunk=col_chunk, tm=tm),
        grid=(T // tm,),
        in_specs=[pl.BlockSpec((tm, D), lambda i: (i, 0)), _resident((D, N)),
                  tab_spec, tab_spec, tab_spec],
        out_specs=out_specs,
        out_shape=out_shape,
        scratch_shapes=[pltpu.VMEM((col_chunk // LANES, tm, LANES), F32)] * min(len(dils), 2),
        compiler_params=_params(("parallel",)),
        name="in_proj",
    )(x2, w_bf, *tables)


def _diffattn_kernel(lam_ref, q_ref, k_ref, vt_ref, g_ref, o_ref, acc_sc,
                     *, tq, tk, unroll, ahead, lam_init):
    qi = pl.program_id(2)
    n_diag = tq // tk
    q = q_ref[0]
    lane = lax.broadcasted_iota(jnp.int32, (tq, LANES), 1)
    zero = jnp.zeros_like(q)
    qs = jnp.concatenate([jnp.where(lane < DA_HEAD_DIM, q, zero),
                          jnp.where(lane >= DA_HEAD_DIM, q, zero)], axis=0)
    acc_sc[...] = jnp.zeros(acc_sc.shape, F32)

    def scores(blk):
        start = pl.multiple_of(blk * tk, tk)
        return lax.dot_general(k_ref[0, pl.ds(start, tk), :], qs, (((1,), (1,)), ((), ())),
                               preferred_element_type=F32)

    def update(s, blk, m_prev, diag):
        if diag is not None:
            krow = diag * tk + lax.broadcasted_iota(jnp.int32, (tk, 2 * tq), 0)
            col = lax.broadcasted_iota(jnp.int32, (tk, 2 * tq), 1)
            s = jnp.where(krow <= jnp.where(col >= tq, col - tq, col), s, NEG_INF)
        m_new = jnp.maximum(m_prev, jnp.max(s, axis=0, keepdims=True))
        alpha = jnp.exp2(m_prev - m_new)
        p = jnp.exp2(s - m_new).astype(BF16)
        pv = jnp.dot(vt_ref[0, 0, blk], p, preferred_element_type=F32)
        acc_sc[...] = alpha * acc_sc[...] + pv
        return m_new

    def run(first_blk, n_plain, m, with_diag):
        n = n_plain + (n_diag if with_diag else 0)
        pending = [scores(first_blk + u) for u in range(min(ahead, n))]
        for u in range(n):
            if u + ahead < n:
                pending.append(scores(first_blk + u + ahead))
            m = update(pending[u], first_blk + u, m, u - n_plain if u >= n_plain else None)
        return m

    n_before = qi * n_diag
    n_groups = n_before // unroll
    m = jnp.full((1, 2 * tq), NEG_INF, F32)
    m = lax.fori_loop(0, n_groups, lambda j, c: run(j * unroll, unroll, c, False), m)
    step = math.gcd(n_diag, unroll)
    tails = [functools.partial(run, n_groups * unroll, r, with_diag=True)
             for r in range(0, unroll, step)]
    lax.switch((n_before - n_groups * unroll) // step, tails, m)

    acc = acc_sc[0:DA_V_DIM, :] / acc_sc[DA_V_DIM:DA_V_DIM + 1, :]
    o = acc[:, :tq] - lam_ref[0] * acc[:, tq:]
    o = o * lax.rsqrt(jnp.mean(o * o, axis=0, keepdims=True) + LN_EPS)
    o_ref[0] = (o.T * g_ref[...] * (1.0 - lam_init)).astype(o_ref.dtype)


def _diff_attention(q, k, vt, lam, sub_g, lam_init, tq=512, unroll=8, ahead=2):
    B, S, _ = q.shape
    tk = DA_TQ
    return pl.pallas_call(
        functools.partial(_diffattn_kernel, tq=tq, tk=tk, unroll=unroll, ahead=ahead,
                          lam_init=lam_init),
        grid=(B, DA_HEADS, S // tq),
        in_specs=[
            pl.BlockSpec(memory_space=pltpu.SMEM),
            pl.BlockSpec((1, tq, LANES), lambda b, h, i: (b, i, h)),
            pl.BlockSpec((1, S, LANES), lambda b, h, i: (b, 0, h)),
            pl.BlockSpec((1, 1, S // tk, DA_VT_ROWS, tk), lambda b, h, i: (b, h, 0, 0, 0)),
            pl.BlockSpec((1, DA_V_DIM), lambda b, h, i: (0, 0)),
        ],
        out_specs=pl.BlockSpec((1, tq, LANES), lambda b, h, i: (b, i, h)),
        out_shape=jax.ShapeDtypeStruct((B, S, DA_WIDTH), BF16),
        scratch_shapes=[pltpu.VMEM((DA_VT_ROWS, 2 * tq), F32)],
        compiler_params=_params(("parallel", "parallel", "arbitrary")),
        name="diff_attn",
    )(lam, q, k, vt, sub_g)


class _HgrnTile:
    def __init__(self, hq, hf, hi, hg, lb, og):
        self.hq, self.hf, self.hg, self.lb, self.og = hq, hf, hg, lb, og
        self.vb = hi.astype(BF16)
        self.tc = hq.shape[0]
        self.n = self.tc // HG_CHUNK

    def gates(self):
        self.q = self.hq * _sigmoid(self.hq)
        f = self.lb + (1.0 - self.lb) * _sigmoid(self.hf)
        self.kk = 1.0 - f
        self.logf = jnp.log(f)

    def decays(self):
        C, n, tc, kw = HG_CHUNK, self.n, self.tc, HG_KWIDTH
        ri = lax.broadcasted_iota(jnp.int32, (tc, tc), 0)
        ci = lax.broadcasted_iota(jnp.int32, (tc, tc), 1)
        self.tril = (ci <= ri) & (ci >= ri - ri % C)
        hi = self.logf.astype(BF16)
        rest = self.logf - hi.astype(F32)
        mid = rest.astype(BF16)
        lo = (rest - mid.astype(F32)).astype(BF16)
        parts = jnp.dot(self.tril.astype(BF16), jnp.concatenate([hi, mid, lo], axis=1),
                        preferred_element_type=F32)
        b = (parts[:, :kw] + parts[:, kw:2 * kw] + parts[:, 2 * kw:]).reshape(n, C, kw)
        b_last = b[:, C - 1:C, :]
        b_mid = b[:, C // 2 - 1:C // 2, :]
        q3 = self.q.reshape(n, C, kw)
        k3 = self.kk.reshape(n, C, kw)
        flat = lambda t: t.reshape(tc, kw).astype(BF16)
        self.qa = flat(q3 * jnp.exp(b - b_mid))
        self.ka = flat(k3 * jnp.exp(b_mid - b))
        self.q_out = flat(q3 * jnp.exp(b))
        self.k_st = flat(k3 * jnp.exp(b_last - b))
        self.decay = jnp.exp(b_last)

    def intra(self):
        self.o = []
        for h in range(HG_HEADS):
            kc = slice(h * HG_KDIM, (h + 1) * HG_KDIM)
            vc = slice(h * HG_VDIM, (h + 1) * HG_VDIM)
            a = lax.dot_general(self.qa[:, kc], self.ka[:, kc], (((1,), (1,)), ((), ())),
                                preferred_element_type=F32)
            a = jnp.where(self.tril, a, 0.0).astype(BF16)
            self.o.append(jnp.dot(a, self.vb[:, vc], preferred_element_type=F32))

    def recur(self, states):
        C = HG_CHUNK
        inter = [[None] * self.n for _ in range(HG_HEADS)]
        for c in range(self.n):
            rows = slice(c * C, (c + 1) * C)
            for h in range(HG_HEADS):
                kc = slice(h * HG_KDIM, (h + 1) * HG_KDIM)
                vc = slice(h * HG_VDIM, (h + 1) * HG_VDIM)
                st = states[h]
                inter[h][c] = lax.dot_general(self.q_out[rows, kc], st.astype(BF16),
                                              (((1,), (1,)), ((), ())),
                                              preferred_element_type=F32)
                upd = lax.dot_general(self.vb[rows, vc], self.k_st[rows, kc],
                                      (((0,), (0,)), ((), ())),
                                      preferred_element_type=F32)
                states = states[:h] + [self.decay[c, :, kc] * st + upd] + states[h + 1:]
        self.o = [o + jnp.concatenate(i, axis=0) for o, i in zip(self.o, inter)]
        return states

    def finish(self, o_ref, rows):
        for h in range(HG_HEADS):
            vc = slice(h * HG_VDIM, (h + 1) * HG_VDIM)
            hg = self.hg[:, vc]
            o = _rms_norm(self.o[h], self.og) * (hg * _sigmoid(hg))
            o_ref[rows, vc] = o.astype(o_ref.dtype)


def _proj0_kernel(x_ref, w_ref, c_ref, s1_ref, s2_ref, lb_ref, og_ref,
                  q_ref, k_ref, vt_ref, ob_ref, state_sc, *, tm, tc, nsb):
    @pl.when(pl.program_id(0) % nsb == 0)
    def _():
        state_sc[...] = jnp.zeros(state_sc.shape, F32)

    xb = x_ref[...].astype(BF16)
    proj = lambda start, width: jnp.dot(xb, w_ref[:, start:start + width],
                                        preferred_element_type=F32)
    g0 = 2 * DA_QK_WIDTH + DA_WIDTH
    hq = proj(g0, HG_KWIDTH)
    hf = proj(g0 + HG_KWIDTH, HG_KWIDTH)
    hi = proj(g0 + 2 * HG_KWIDTH, HG_WIDTH)
    hg = proj(g0 + 2 * HG_KWIDTH + HG_WIDTH, HG_WIDTH)
    lb, og = lb_ref[...], og_ref[...]
    tiles = [_HgrnTile(hq[r:r + tc], hf[r:r + tc], hi[r:r + tc], hg[r:r + tc], lb, og)
             for r in range(0, tm, tc)]

    cw = 2 * LANES

    def emit_qk(out_ref, start, j, scale):
        acc = _rope(proj(start + j, cw), c_ref[...], s1_ref[...], s2_ref[...], scale)
        out_ref[:, j:j + cw] = acc.astype(out_ref.dtype)

    def emit_vt(j):
        acc = proj(2 * DA_QK_WIDTH + j, cw)
        for g in range(cw // DA_V_DIM):
            head = j // DA_V_DIM + g
            for sb in range(tm // DA_TQ):
                t = acc[sb * DA_TQ:(sb + 1) * DA_TQ, g * DA_V_DIM:(g + 1) * DA_V_DIM].T
                vt_ref[0, head, sb, 0:DA_V_DIM, :] = t.astype(vt_ref.dtype)
                vt_ref[0, head, sb, DA_V_DIM:, :] = jnp.ones(
                    (DA_VT_ROWS - DA_V_DIM, DA_TQ), vt_ref.dtype)

    big = ([functools.partial(emit_qk, q_ref, 0, j, Q_SCALE) for j in range(0, DA_QK_WIDTH, cw)]
           + [functools.partial(emit_qk, k_ref, DA_QK_WIDTH, j, None)
              for j in range(0, DA_QK_WIDTH, cw)]
           + [functools.partial(emit_vt, j) for j in range(0, DA_WIDTH, cw)])
    states = [state_sc[h] for h in range(HG_HEADS)]

    def interleaved(phase):
        if big:
            big.pop(0)()
        return phase()

    for t in tiles:
        t.gates()
    for t in tiles:
        interleaved(t.decays)
    for t in tiles:
        interleaved(t.intra)
    for t in tiles:
        states = interleaved(functools.partial(t.recur, states))
    for n, t in enumerate(tiles):
        interleaved(functools.partial(t.finish, ob_ref, slice(n * tc, (n + 1) * tc)))
    while big:
        big.pop(0)()
    for h in range(HG_HEADS):
        state_sc[h] = states[h]


def _project_layer0(x2, w_bf, tables, lb, out_g, seq, tm=512, tc=256):
    T, D = x2.shape
    N = w_bf.shape[1]
    nsb = seq // tm
    rows = lambda w: pl.BlockSpec((tm, w), lambda i: (i, 0))
    tab_spec = pl.BlockSpec((tm, LANES), lambda i: (i % nsb, 0))
    vt_shape = (T // seq, DA_HEADS, seq // DA_TQ, DA_VT_ROWS, DA_TQ)
    return pl.pallas_call(
        functools.partial(_proj0_kernel, tm=tm, tc=tc, nsb=nsb),
        grid=(T // tm,),
        in_specs=[rows(D), _resident((D, N)), tab_spec, tab_spec, tab_spec,
                  _resident((1, HG_KWIDTH)), _resident((1, HG_VDIM))],
        out_specs=[rows(DA_QK_WIDTH), rows(DA_QK_WIDTH),
                   pl.BlockSpec((1, DA_HEADS, tm // DA_TQ, DA_VT_ROWS, DA_TQ),
                                lambda i: (i // nsb, 0, i % nsb, 0, 0)),
                   rows(HG_WIDTH)],
        out_shape=[jax.ShapeDtypeStruct((T, DA_QK_WIDTH), BF16),
                   jax.ShapeDtypeStruct((T, DA_QK_WIDTH), BF16),
                   jax.ShapeDtypeStruct(vt_shape, BF16),
                   jax.ShapeDtypeStruct((T, HG_WIDTH), BF16)],
        scratch_shapes=[pltpu.VMEM((HG_HEADS, HG_VDIM, HG_KDIM), F32)],
        compiler_params=_params(("arbitrary",)),
        name="in_proj_hgrn2",
    )(x2, w_bf, *tables, lb, out_g)


def _dil_kernel(*refs, tl, n_sub, ahead, first, last):
    q_ref, kp_ref, kc_ref, vp_ref, vc_ref = refs[:5]
    refs = refs[5:]
    if not first:
        accp_ref, mlp_ref = refs[:2]
        refs = refs[2:]
    if last:
        o_ref = refs[0]
        refs = refs[1:]
    else:
        acc_ref, ml_ref = refs[:2]
        refs = refs[2:]
    i = pl.program_id(2)
    n_pairs = DL_HEADS // 2
    half = DL_HEAD_DIM

    if not first:
        acc_sc, ml_sc = refs
        quarter = n_sub * tl // 4
        for c in range(4):
            for hp in range(n_pairs):
                acc_sc[hp, pl.ds(c, quarter, stride=4), :] = (
                    accp_ref[0, c, 0, :, hp * LANES:(hp + 1) * LANES])
            ml_sc[pl.ds(c, quarter, stride=4), :] = mlp_ref[0, c, 0]

    lane = lax.broadcasted_iota(jnp.int32, (tl, LANES), 1)
    low = lane < half

    def pair_rows(x):
        return jnp.concatenate([jnp.broadcast_to(x[:, :tl], (half, tl)),
                                jnp.broadcast_to(x[:, tl:], (half, tl))], axis=0)

    def window(prev_ref, cur_ref, n, cols):
        if n == 0:
            return jnp.concatenate([prev_ref[0, 0, :, cols], cur_ref[0, 0, 0:tl, cols]], axis=0)
        return cur_ref[0, 0, (n - 1) * tl:(n + 1) * tl, cols]

    def rows_of(n):
        return slice(n * tl, (n + 1) * tl)

    def scores(n, hp):
        cols = slice(hp * LANES, (hp + 1) * LANES)
        qp = q_ref[0, 0, rows_of(n), cols]
        zero = jnp.zeros_like(qp)
        qs = jnp.concatenate([jnp.where(low, qp, zero), jnp.where(low, zero, qp)], axis=0)
        return lax.dot_general(window(kp_ref, kc_ref, n, cols), qs, (((1,), (1,)), ((), ())),
                               preferred_element_type=F32)

    def update(s, n, hp, ml_t):
        cols = slice(hp * LANES, (hp + 1) * LANES)
        key = lax.broadcasted_iota(jnp.int32, (2 * tl, 2 * tl), 0)
        qcol = lax.broadcasted_iota(jnp.int32, (2 * tl, 2 * tl), 1)
        dist = tl + jnp.where(qcol >= tl, qcol - tl, qcol) - key
        valid = (dist >= 0) & (dist <= DL_SPAN)
        if n == 0:
            valid = valid & ((key >= tl) | (i > 0))
        s = jnp.where(valid, s, NEG_INF)
        m_new = jnp.max(s, axis=0, keepdims=True)
        if not first:
            h0 = 2 * hp
            m_prev = jnp.concatenate([ml_t[h0:h0 + 1], ml_t[h0 + 1:h0 + 2]], axis=1)
            l_prev = jnp.concatenate([ml_t[DL_HEADS + h0:DL_HEADS + h0 + 1],
                                      ml_t[DL_HEADS + h0 + 1:DL_HEADS + h0 + 2]], axis=1)
            m_new = jnp.maximum(m_prev, m_new)
            alpha = jnp.exp2(m_prev - m_new)
        p = jnp.exp2(s - m_new)
        l_new = jnp.sum(p, axis=0, keepdims=True)
        pv = lax.dot_general(window(vp_ref, vc_ref, n, cols), p.astype(BF16),
                             (((0,), (0,)), ((), ())), preferred_element_type=F32)
        o_t = jnp.concatenate([pv[:half, :tl], pv[half:, tl:]], axis=0)
        if not first:
            l_new = l_new + alpha * l_prev
            o_t = o_t + pair_rows(alpha) * acc_sc[hp, rows_of(n), :].T
        if last:
            o_ref[0, 0, rows_of(n), cols] = (o_t * pair_rows(1.0 / l_new)).T.astype(o_ref.dtype)
        else:
            acc_ref[0, 0, rows_of(n), cols] = o_t.T
        return m_new, l_new

    blocks = [(n, hp) for n in range(n_sub) for hp in range(n_pairs)]
    pending = [scores(*blk) for blk in blocks[:ahead]]
    for idx, (n, hp) in enumerate(blocks):
        if idx + ahead < len(blocks):
            pending.append(scores(*blocks[idx + ahead]))
        if hp == 0:
            ml_t = None if first else ml_sc[rows_of(n), :].T
            ms, ls = [], []
        m_new, l_new = update(pending[idx], n, hp, ml_t)
        pending[idx] = None
        ms += [m_new[:, :tl], m_new[:, tl:]]
        ls += [l_new[:, :tl], l_new[:, tl:]]
        if hp == n_pairs - 1 and not last:
            pad = jnp.zeros((LANES - 2 * DL_HEADS, tl), F32)
            ml_ref[0, 0, rows_of(n), :] = jnp.concatenate(ms + ls + [pad], axis=0).T


def _dilated_stage(q, k, v, dil, state, last, tl=DL_SPAN, max_sub=4, ahead=4):
    B, _, L, D = q.shape
    first = state is None
    n_sub = min(max_sub, L // tl)
    TL = n_sub * tl
    cur = lambda b, r, i: (b, r, i, 0)
    prev = lambda b, r, i: (b, r, jnp.maximum(i * n_sub - 1, 0), 0)
    tile = pl.BlockSpec((1, 1, TL, D), cur)
    halo = pl.BlockSpec((1, 1, tl, D), prev)
    in_specs = [tile, halo, tile, halo, tile]
    args = [q, k, k, v, v]
    scratch = []
    if not first:
        fine = lambda b, r, i: (b, 0, r, i, 0)
        in_specs += [pl.BlockSpec((1, 4, 1, TL // 4, D), fine),
                     pl.BlockSpec((1, 4, 1, TL // 4, LANES), fine)]
        args += [state[0].reshape(B, 4, dil, L // 4, D), state[1].reshape(B, 4, dil, L // 4, LANES)]
        scratch = [pltpu.VMEM((D // LANES, TL, LANES), F32), pltpu.VMEM((TL, LANES), F32)]
    if last:
        out_shape = [jax.ShapeDtypeStruct((B, dil, L, D), BF16)]
        out_specs = [tile]
    else:
        out_shape = [jax.ShapeDtypeStruct((B, dil, L, D), F32),
                     jax.ShapeDtypeStruct((B, dil, L, LANES), F32)]
        out_specs = [tile, pl.BlockSpec((1, 1, TL, LANES), cur)]
    outs = pl.pallas_call(
        functools.partial(_dil_kernel, tl=tl, n_sub=n_sub, ahead=ahead, first=first, last=last),
        grid=(B, dil, L // TL),
        in_specs=in_specs,
        out_specs=out_specs,
        out_shape=out_shape,
        scratch_shapes=scratch,
        compiler_params=_params(("parallel", "parallel", "arbitrary")),
        name=f"dilated_attn_d{dil}",
    )(*args)
    return outs[0] if last else tuple(outs)


def _dilated_attention(qkv_by_dil):
    dils = sorted((d for _, d in DL_PAIRS), reverse=True)
    assert all(w // d == DL_SPAN for w, d in DL_PAIRS)
    assert all(a == 4 * b for a, b in zip(dils, dils[1:])) and dils[-1] == 1
    state = None
    for d in dils:
        state = _dilated_stage(*qkv_by_dil[d], d, state, last=(d == 1))
    return state


def _tail_kernel(*refs, n_mix, ffn_chunk):
    x_ref = refs[0]
    mix_refs = refs[1:1 + n_mix]
    wo_refs = refs[1 + n_mix:1 + 2 * n_mix]
    (g1_ref, b1_ref, w1_ref, w2_ref, g2_ref, b2_ref,
     p_ref, wp_ref, wg_ref, pg_ref, o_ref) = refs[1 + 2 * n_mix:]
    mix = jnp.dot(mix_refs[0][...], wo_refs[0][...], preferred_element_type=F32)
    for m_ref, w_ref in zip(mix_refs[1:], wo_refs[1:]):
        mix = mix + jnp.dot(m_ref[...], w_ref[...], preferred_element_type=F32)
    x1 = _layer_norm(ALPHA * x_ref[...] + mix, g1_ref[...], b1_ref[...])
    x1b = x1.astype(BF16)
    ffn = None
    for c in range(0, FFN_DIM, ffn_chunk):
        hdn = jnp.dot(x1b, w1_ref[:, c:c + ffn_chunk], preferred_element_type=F32)
        hdn = jnp.square(jnp.maximum(hdn, 0.0)).astype(BF16)
        part = jnp.dot(hdn, w2_ref[c:c + ffn_chunk, :], preferred_element_type=F32)
        ffn = part if ffn is None else ffn + part
    x2 = _layer_norm(ALPHA * x1 + ffn, g2_ref[...], b2_ref[...])
    e = _rms_norm(jnp.dot(p_ref[...].astype(BF16), wp_ref[...], preferred_element_type=F32),
                  pg_ref[...])
    gate = _sigmoid(jnp.dot(x2.astype(BF16), wg_ref[...], preferred_element_type=F32))
    o_ref[...] = x2 + gate * e


def _layer_tail(x2, mixes, w_outs, g1, b1, w1, w2, g2, b2, p2, wp, wg, pg, tm=512,
                ffn_chunk=1024):
    T, D = x2.shape
    rows = lambda w: pl.BlockSpec((tm, w), lambda i: (i, 0))
    vec = lambda a: a.reshape(1, -1)
    in_specs = ([rows(D)] + [rows(m.shape[1]) for m in mixes]
                + [_resident(w.shape) for w in w_outs]
                + [_resident((1, D)), _resident((1, D)), _resident(w1.shape),
                   _resident(w2.shape), _resident((1, D)), _resident((1, D)),
                   rows(PLE_DIM), _resident(wp.shape), _resident(wg.shape),
                   _resident((1, D))])
    return pl.pallas_call(
        functools.partial(_tail_kernel, n_mix=len(mixes), ffn_chunk=ffn_chunk),
        grid=(T // tm,),
        in_specs=in_specs,
        out_specs=rows(D),
        out_shape=jax.ShapeDtypeStruct((T, D), F32),
        compiler_params=_params(("parallel",)),
        name="layer_tail",
    )(x2, *mixes, *w_outs, vec(g1), vec(b1), w1, w2, vec(g2), vec(b2), p2, wp, wg, vec(pg))


def kernel(x, p, ev_w_in, ev_w_out, da_lambda, da_subln_g, hg_lb_logits, hg_norm_g,
           od_w_in, od_w_out, ln1_g, ln1_b, ffn_w1, ffn_w2, ln2_g, ln2_b,
           ple_w_proj, ple_w_gate, ple_norm_g):
    B, S, D = x.shape
    T = B * S
    tables = _rope_lane_tables(S)
    lb_all = jnp.cumsum(jax.nn.softmax(hg_lb_logits.astype(F32), axis=0), axis=0)
    bf = lambda w: w.astype(BF16)
    x2 = x.reshape(T, D)
    for l in range(DEPTH):
        j = l // 2
        if l % 2 == 0:
            lam_init = 0.8 - 0.6 * math.exp(-0.3 * l)
            lp = da_lambda[j].astype(F32)
            lam = (jnp.exp(jnp.sum(lp[0] * lp[1])) - jnp.exp(jnp.sum(lp[2] * lp[3]))
                   + lam_init).reshape(1)
            q, k, vt, o_b = _project_layer0(x2, bf(ev_w_in[j]), tables, lb_all[j].reshape(1, -1),
                                            hg_norm_g[j].reshape(1, -1), S)
            o_a = _diff_attention(q.reshape(B, S, -1), k.reshape(B, S, -1), vt,
                                  lam, da_subln_g[j].reshape(1, -1), lam_init)
            mixes = [o_a.reshape(T, -1), o_b.reshape(T, -1)]
            w_o = bf(ev_w_out[j])
            w_outs = [w_o[:DA_WIDTH], w_o[DA_WIDTH:]]
        else:
            segs = ((0, D, "q"), (D, D, "k"), (2 * D, D, "plain"))
            coarse = tuple(sorted(d for _, d in DL_PAIRS if d > 1))
            outs = _project(x2, bf(od_w_in[j]), tables, S, segs, coarse)
            per = 1 + len(coarse)
            qkv = {1: tuple(outs[n * per].reshape(B, 1, S, D) for n in range(3))}
            for m, d in enumerate(coarse):
                qkv[d] = tuple(outs[n * per + 1 + m] for n in range(3))
            o = _dilated_attention(qkv)
            mixes = [o.reshape(T, D)]
            w_outs = [bf(od_w_out[j])]
        x2 = _layer_tail(x2, mixes, w_outs, ln1_g[l], ln1_b[l], bf(ffn_w1[l]), bf(ffn_w2[l]),
                         ln2_g[l], ln2_b[l], p[l].reshape(T, PLE_DIM), bf(ple_w_proj[l]),
                         bf(ple_w_gate[l]), ple_norm_g[l])
    return x2.reshape(B, S, D)
```

```python
import functools
import math

import jax
import jax.numpy as jnp
from jax import lax
from jax.experimental import pallas as pl
from jax.experimental.pallas import tpu as pltpu

D_MODEL = 1024
DEPTH = 2
PLE_DIM = 256

DA_HEADS = 4
DA_HEAD_DIM = 64
DA_V_DIM = 2 * DA_HEAD_DIM
DA_QK_WIDTH = DA_HEADS * 2 * DA_HEAD_DIM
DA_WIDTH = DA_HEADS * DA_V_DIM
DA_TQ = 256
DA_VT_ROWS = DA_V_DIM + 16
LOG2_E = math.log2(math.e)

HG_HEADS = 4
HG_KDIM = 128
HG_VDIM = 128
HG_KWIDTH = HG_HEADS * HG_KDIM
HG_WIDTH = HG_HEADS * HG_VDIM
HG_CHUNK = 64

DL_HEADS = 16
DL_HEAD_DIM = D_MODEL // DL_HEADS
DL_PAIRS = ((128, 1), (512, 4), (2048, 16))
DL_SPAN = 128

FFN_DIM = 4 * D_MODEL
ROPE_THETA = 500000.0
ROT_DIM = 16
ALPHA = (2 * DEPTH) ** 0.25
LN_EPS = 1e-5

LANES = 128
VMEM_LIMIT_BYTES = 56 * 1024 * 1024

F32 = jnp.float32
BF16 = jnp.bfloat16
NEG_INF = float("-inf")


def _params(sem, vmem=VMEM_LIMIT_BYTES):
    return pltpu.CompilerParams(dimension_semantics=sem, vmem_limit_bytes=vmem)


def _resident(shape):
    nd = len(shape)
    return pl.BlockSpec(shape, lambda *_: (0,) * nd, pipeline_mode=pl.Buffered(1))


def _sigmoid(x):
    return 1.0 / (1.0 + jnp.exp(-x))


def _layer_norm(x, g, b):
    mu = jnp.mean(x, axis=-1, keepdims=True)
    xc = x - mu
    var = jnp.mean(xc * xc, axis=-1, keepdims=True)
    return xc * lax.rsqrt(var + LN_EPS) * g + b


def _rms_norm(x, g):
    return x * lax.rsqrt(jnp.mean(x * x, axis=-1, keepdims=True) + LN_EPS) * g


def _rope_lane_tables(seq):
    half = ROT_DIM // 2
    inv = ROPE_THETA ** (-jnp.arange(0, ROT_DIM, 2, dtype=F32) / ROT_DIM)
    lane = jnp.arange(LANES) % 64
    inv_lane = jnp.where(lane < ROT_DIM, inv[lane % half], 0.0)
    ang = jnp.arange(seq, dtype=F32)[:, None] * inv_lane[None, :]
    cos, sin = jnp.cos(ang), jnp.sin(ang)
    s1 = jnp.where((lane >= half) & (lane < ROT_DIM), sin, 0.0)
    s2 = jnp.where(lane < half, -sin, 0.0)
    return cos, s1, s2


def _rope(acc, c, s1, s2, scale):
    parts = []
    for g in range(acc.shape[1] // LANES):
        a = acc[:, g * LANES:(g + 1) * LANES]
        r = (a * c + pltpu.roll(a, ROT_DIM // 2, 1) * s1
             + pltpu.roll(a, LANES - ROT_DIM // 2, 1) * s2)
        parts.append(r if scale is None else r * scale)
    return jnp.concatenate(parts, axis=1)


Q_SCALE = 64 ** -0.5 * LOG2_E


def _proj_kernel(x_ref, w_ref, c_ref, s1_ref, s2_ref, *refs, segs, dils, col_chunk, tm):
    n_out = len(segs) * (1 + len(dils))
    out_refs, stage = refs[:n_out], refs[n_out:]
    xb = x_ref[...].astype(BF16)
    c, s1, s2 = c_ref[...], s1_ref[...], s2_ref[...]
    for n, (start, width, kind) in enumerate(segs):
        nat_ref = out_refs[n * (1 + len(dils))]
        dil_refs = out_refs[n * (1 + len(dils)) + 1:(n + 1) * (1 + len(dils))]
        for j in range(0, width, col_chunk):
            acc = jnp.dot(xb, w_ref[:, start + j:start + j + col_chunk],
                          preferred_element_type=F32)
            if kind != "plain":
                acc = _rope(acc, c, s1, s2, {"q": Q_SCALE, "k": None}[kind])
            nat_ref[:, j:j + col_chunk] = acc.astype(nat_ref.dtype)
            for g in range(col_chunk // LANES):
                cols = slice(j + g * LANES, j + (g + 1) * LANES)
                src, d_prev = stage[0], 1
                src[g] = acc[:, g * LANES:(g + 1) * LANES]
                for n_d, (d, ref) in enumerate(zip(dils, dil_refs)):
                    ratio, rows = d // d_prev, tm // d
                    dst = stage[(n_d + 1) % 2] if n_d + 1 < len(dils) else None
                    for r_prev in range(d_prev):
                        for c_new in range(ratio):
                            r = r_prev + d_prev * c_new
                            piece = src[g, pl.ds(r_prev * (tm // d_prev) + c_new, rows,
                                                 stride=ratio), :]
                            ref[0, r, :, cols] = piece.astype(ref.dtype)
                            if dst is not None:
                                dst[g, r * rows:(r + 1) * rows, :] = piece
                    src, d_prev = dst, d


def _project(x2, w_bf, tables, seq, segs, dils, tm=512, col_chunk=512):
    T, D = x2.shape
    N = w_bf.shape[1]
    nsb = seq // tm
    assert dils and all(b % a == 0 for a, b in zip((1,) + tuple(dils), dils))
    tab_spec = pl.BlockSpec((tm, LANES), lambda i: (i % nsb, 0))
    out_shape, out_specs = [], []
    for (_, w, _) in segs:
        out_shape.append(jax.ShapeDtypeStruct((T, w), BF16))
        out_specs.append(pl.BlockSpec((tm, w), lambda i: (i, 0)))
        for d in dils:
            out_shape.append(jax.ShapeDtypeStruct((T // seq, d, seq // d, w), BF16))
            out_specs.append(pl.BlockSpec((1, d, tm // d, w), lambda i: (i // nsb, 0, i % nsb, 0)))
    return pl.pallas_call(
        functools.partial(_proj_kernel, segs=segs, dils=dils, col_chunk=col_chunk, tm=tm),
        grid=(T // tm,),
        in_specs=[pl.BlockSpec((tm, D), lambda i: (i, 0)), _resident((D, N)),
                  tab_spec, tab_spec, tab_spec],
        out_specs=out_specs,
        out_shape=out_shape,
        scratch_shapes=[pltpu.VMEM((col_chunk // LANES, tm, LANES), F32)] * min(len(dils), 2),
        compiler_params=_params(("parallel",)),
        name="in_proj",
    )(x2, w_bf, *tables)


def _diffattn_kernel(lam_ref, q_ref, k_ref, vt_ref, g_ref, o_ref, acc_sc,
                     *, tq, tk, unroll, ahead, lam_init):
    qi = pl.program_id(2)
    n_diag = tq // tk
    q = q_ref[0]
    lane = lax.broadcasted_iota(jnp.int32, (tq, LANES), 1)
    zero = jnp.zeros_like(q)
    qs = jnp.concatenate([jnp.where(lane < DA_HEAD_DIM, q, zero),
                          jnp.where(lane >= DA_HEAD_DIM, q, zero)], axis=0)
    acc_sc[...] = jnp.zeros(acc_sc.shape, F32)

    def scores(blk):
        start = pl.multiple_of(blk * tk, tk)
        return lax.dot_general(k_ref[0, pl.ds(start, tk), :], qs, (((1,), (1,)), ((), ())),
                               preferred_element_type=F32)

    def update(s, blk, m_prev, diag):
        if diag is not None:
            krow = diag * tk + lax.broadcasted_iota(jnp.int32, (tk, 2 * tq), 0)
            col = lax.broadcasted_iota(jnp.int32, (tk, 2 * tq), 1)
            s = jnp.where(krow <= jnp.where(col >= tq, col - tq, col), s, NEG_INF)
        m_new = jnp.maximum(m_prev, jnp.max(s, axis=0, keepdims=True))
        alpha = jnp.exp2(m_prev - m_new)
        p = jnp.exp2(s - m_new).astype(BF16)
        pv = jnp.dot(vt_ref[0, 0, blk], p, preferred_element_type=F32)
        acc_sc[...] = alpha * acc_sc[...] + pv
        return m_new

    def run(first_blk, n_plain, m, with_diag):
        n = n_plain + (n_diag if with_diag else 0)
        pending = [scores(first_blk + u) for u in range(min(ahead, n))]
        for u in range(n):
            if u + ahead < n:
                pending.append(scores(first_blk + u + ahead))
            m = update(pending[u], first_blk + u, m, u - n_plain if u >= n_plain else None)
        return m

    n_before = qi * n_diag
    n_groups = n_before // unroll
    m = jnp.full((1, 2 * tq), NEG_INF, F32)
    m = lax.fori_loop(0, n_groups, lambda j, c: run(j * unroll, unroll, c, False), m)
    step = math.gcd(n_diag, unroll)
    tails = [functools.partial(run, n_groups * unroll, r, with_diag=True)
             for r in range(0, unroll, step)]
    lax.switch((n_before - n_groups * unroll) // step, tails, m)

    acc = acc_sc[0:DA_V_DIM, :] / acc_sc[DA_V_DIM:DA_V_DIM + 1, :]
    o = acc[:, :tq] - lam_ref[0] * acc[:, tq:]
    o = o * lax.rsqrt(jnp.mean(o * o, axis=0, keepdims=True) + LN_EPS)
    o_ref[0] = (o.T * g_ref[...] * (1.0 - lam_init)).astype(o_ref.dtype)


def _diff_attention(q, k, vt, lam, sub_g, lam_init, tq=512, unroll=16, ahead=2):
    B, S, _ = q.shape
    tk = DA_TQ
    return pl.pallas_call(
        functools.partial(_diffattn_kernel, tq=tq, tk=tk, unroll=unroll, ahead=ahead,
                          lam_init=lam_init),
        grid=(B, DA_HEADS, S // tq),
        in_specs=[
            pl.BlockSpec(memory_space=pltpu.SMEM),
            pl.BlockSpec((1, tq, LANES), lambda b, h, i: (b, i, h)),
            pl.BlockSpec((1, S, LANES), lambda b, h, i: (b, 0, h)),
            pl.BlockSpec((1, 1, S // tk, DA_VT_ROWS, tk), lambda b, h, i: (b, h, 0, 0, 0)),
            pl.BlockSpec((1, DA_V_DIM), lambda b, h, i: (0, 0)),
        ],
        out_specs=pl.BlockSpec((1, tq, LANES), lambda b, h, i: (b, i, h)),
        out_shape=jax.ShapeDtypeStruct((B, S, DA_WIDTH), BF16),
        scratch_shapes=[pltpu.VMEM((DA_VT_ROWS, 2 * tq), F32)],
        compiler_params=_params(("parallel", "parallel", "arbitrary")),
        name="diff_attn",
    )(lam, q, k, vt, sub_g)


class _HgrnTile:
    def __init__(self, hq, hf, hi, hg, lb, og):
        self.hq, self.hf, self.hg, self.lb, self.og = hq, hf, hg, lb, og
        self.vb = hi.astype(BF16)
        self.tc = hq.shape[0]
        self.n = self.tc // HG_CHUNK

    def gates(self):
        self.q = self.hq * _sigmoid(self.hq)
        f = self.lb + (1.0 - self.lb) * _sigmoid(self.hf)
        self.kk = 1.0 - f
        self.logf = jnp.log(f)

    def decays(self):
        C, n, tc, kw = HG_CHUNK, self.n, self.tc, HG_KWIDTH
        ri = lax.broadcasted_iota(jnp.int32, (tc, tc), 0)
        ci = lax.broadcasted_iota(jnp.int32, (tc, tc), 1)
        self.tril = (ci <= ri) & (ci >= ri - ri % C)
        hi = self.logf.astype(BF16)
        rest = self.logf - hi.astype(F32)
        mid = rest.astype(BF16)
        lo = (rest - mid.astype(F32)).astype(BF16)
        parts = jnp.dot(self.tril.astype(BF16), jnp.concatenate([hi, mid, lo], axis=1),
                        preferred_element_type=F32)
        b = (parts[:, :kw] + parts[:, kw:2 * kw] + parts[:, 2 * kw:]).reshape(n, C, kw)
        b_last = b[:, C - 1:C, :]
        b_mid = b[:, C // 2 - 1:C // 2, :]
        q3 = self.q.reshape(n, C, kw)
        k3 = self.kk.reshape(n, C, kw)
        flat = lambda t: t.reshape(tc, kw).astype(BF16)
        self.qa = flat(q3 * jnp.exp(b - b_mid))
        self.ka = flat(k3 * jnp.exp(b_mid - b))
        self.q_out = flat(q3 * jnp.exp(b))
        self.k_st = flat(k3 * jnp.exp(b_last - b))
        self.decay = jnp.exp(b_last)

    def intra(self):
        self.o = []
        for h in range(HG_HEADS):
            kc = slice(h * HG_KDIM, (h + 1) * HG_KDIM)
            vc = slice(h * HG_VDIM, (h + 1) * HG_VDIM)
            a = lax.dot_general(self.qa[:, kc], self.ka[:, kc], (((1,), (1,)), ((), ())),
                                preferred_element_type=F32)
            a = jnp.where(self.tril, a, 0.0).astype(BF16)
            self.o.append(jnp.dot(a, self.vb[:, vc], preferred_element_type=F32))

    def recur(self, states):
        C = HG_CHUNK
        chunk = lambda c: slice(c * C, (c + 1) * C)
        kcol = lambda h: slice(h * HG_KDIM, (h + 1) * HG_KDIM)
        vcol = lambda h: slice(h * HG_VDIM, (h + 1) * HG_VDIM)
        upd = [[lax.dot_general(self.vb[chunk(c), vcol(h)], self.k_st[chunk(c), kcol(h)],
                                (((0,), (0,)), ((), ())), preferred_element_type=F32)
                for c in range(self.n)] for h in range(HG_HEADS)]
        before = [[None] * self.n for _ in range(HG_HEADS)]
        for h in range(HG_HEADS):
            st = states[h]
            for c in range(self.n):
                before[h][c] = st.astype(BF16)
                st = self.decay[c, :, kcol(h)] * st + upd[h][c]
            states = states[:h] + [st] + states[h + 1:]
        for h in range(HG_HEADS):
            inter = [lax.dot_general(self.q_out[chunk(c), kcol(h)], before[h][c],
                                     (((1,), (1,)), ((), ())), preferred_element_type=F32)
                     for c in range(self.n)]
            self.o[h] = self.o[h] + jnp.concatenate(inter, axis=0)
        return states

    def finish(self, o_ref, rows):
        for h in range(HG_HEADS):
            vc = slice(h * HG_VDIM, (h + 1) * HG_VDIM)
            hg = self.hg[:, vc]
            o = _rms_norm(self.o[h], self.og) * (hg * _sigmoid(hg))
            o_ref[rows, vc] = o.astype(o_ref.dtype)


def _proj0_kernel(x_ref, w_ref, c_ref, s1_ref, s2_ref, lb_ref, og_ref,
                  q_ref, k_ref, vt_ref, ob_ref, state_sc, *, tm, tc, nsb):
    @pl.when(pl.program_id(0) % nsb == 0)
    def _():
        state_sc[...] = jnp.zeros(state_sc.shape, F32)

    xb = x_ref[...].astype(BF16)
    proj = lambda start, width: jnp.dot(xb, w_ref[:, start:start + width],
                                        preferred_element_type=F32)
    g0 = 2 * DA_QK_WIDTH + DA_WIDTH
    hq = proj(g0, HG_KWIDTH)
    hf = proj(g0 + HG_KWIDTH, HG_KWIDTH)
    hi = proj(g0 + 2 * HG_KWIDTH, HG_WIDTH)
    hg = proj(g0 + 2 * HG_KWIDTH + HG_WIDTH, HG_WIDTH)
    lb, og = lb_ref[...], og_ref[...]
    tiles = [_HgrnTile(hq[r:r + tc], hf[r:r + tc], hi[r:r + tc], hg[r:r + tc], lb, og)
             for r in range(0, tm, tc)]

    cw = 2 * LANES

    def emit_qk(out_ref, start, j, scale):
        acc = _rope(proj(start + j, cw), c_ref[...], s1_ref[...], s2_ref[...], scale)
        out_ref[:, j:j + cw] = acc.astype(out_ref.dtype)

    def emit_vt(j):
        acc = proj(2 * DA_QK_WIDTH + j, cw)
        for g in range(cw // DA_V_DIM):
            head = j // DA_V_DIM + g
            for sb in range(tm // DA_TQ):
                t = acc[sb * DA_TQ:(sb + 1) * DA_TQ, g * DA_V_DIM:(g + 1) * DA_V_DIM].T
                vt_ref[0, head, sb, 0:DA_V_DIM, :] = t.astype(vt_ref.dtype)
                vt_ref[0, head, sb, DA_V_DIM:, :] = jnp.ones(
                    (DA_VT_ROWS - DA_V_DIM, DA_TQ), vt_ref.dtype)

    big = ([functools.partial(emit_qk, q_ref, 0, j, Q_SCALE) for j in range(0, DA_QK_WIDTH, cw)]
           + [functools.partial(emit_qk, k_ref, DA_QK_WIDTH, j, None)
              for j in range(0, DA_QK_WIDTH, cw)]
           + [functools.partial(emit_vt, j) for j in range(0, DA_WIDTH, cw)])
    states = [state_sc[h] for h in range(HG_HEADS)]

    def interleaved(phase):
        if big:
            big.pop(0)()
        return phase()

    for t in tiles:
        t.gates()
    for t in tiles:
        interleaved(t.decays)
    for t in tiles:
        interleaved(t.intra)
    for t in tiles:
        states = interleaved(functools.partial(t.recur, states))
    for n, t in enumerate(tiles):
        interleaved(functools.partial(t.finish, ob_ref, slice(n * tc, (n + 1) * tc)))
    while big:
        big.pop(0)()
    for h in range(HG_HEADS):
        state_sc[h] = states[h]


def _project_layer0(x2, w_bf, tables, lb, out_g, seq, tm=512, tc=256):
    T, D = x2.shape
    N = w_bf.shape[1]
    nsb = seq // tm
    rows = lambda w: pl.BlockSpec((tm, w), lambda i: (i, 0))
    tab_spec = pl.BlockSpec((tm, LANES), lambda i: (i % nsb, 0))
    vt_shape = (T // seq, DA_HEADS, seq // DA_TQ, DA_VT_ROWS, DA_TQ)
    return pl.pallas_call(
        functools.partial(_proj0_kernel, tm=tm, tc=tc, nsb=nsb),
        grid=(T // tm,),
        in_specs=[rows(D), _resident((D, N)), tab_spec, tab_spec, tab_spec,
                  _resident((1, HG_KWIDTH)), _resident((1, HG_VDIM))],
        out_specs=[rows(DA_QK_WIDTH), rows(DA_QK_WIDTH),
                   pl.BlockSpec((1, DA_HEADS, tm // DA_TQ, DA_VT_ROWS, DA_TQ),
                                lambda i: (i // nsb, 0, i % nsb, 0, 0)),
                   rows(HG_WIDTH)],
        out_shape=[jax.ShapeDtypeStruct((T, DA_QK_WIDTH), BF16),
                   jax.ShapeDtypeStruct((T, DA_QK_WIDTH), BF16),
                   jax.ShapeDtypeStruct(vt_shape, BF16),
                   jax.ShapeDtypeStruct((T, HG_WIDTH), BF16)],
        scratch_shapes=[pltpu.VMEM((HG_HEADS, HG_VDIM, HG_KDIM), F32)],
        compiler_params=_params(("arbitrary",)),
        name="in_proj_hgrn2",
    )(x2, w_bf, *tables, lb, out_g)


def _dil_kernel(*refs, tl, n_sub, ahead, first, last):
    q_ref, kp_ref, kc_ref, vp_ref, vc_ref = refs[:5]
    refs = refs[5:]
    if not first:
        accp_ref, mlp_ref = refs[:2]
        refs = refs[2:]
    if last:
        o_ref = refs[0]
        refs = refs[1:]
    else:
        acc_ref, ml_ref = refs[:2]
        refs = refs[2:]
    i = pl.program_id(2)
    n_pairs = DL_HEADS // 2
    half = DL_HEAD_DIM

    if not first:
        acc_sc, ml_sc = refs
        quarter = n_sub * tl // 4
        for c in range(4):
            for hp in range(n_pairs):
                acc_sc[hp, pl.ds(c, quarter, stride=4), :] = (
                    accp_ref[0, c, 0, :, hp * LANES:(hp + 1) * LANES])
            ml_sc[pl.ds(c, quarter, stride=4), :] = mlp_ref[0, c, 0]

    lane = lax.broadcasted_iota(jnp.int32, (tl, LANES), 1)
    low = lane < half

    def pair_rows(x):
        return jnp.concatenate([jnp.broadcast_to(x[:, :tl], (half, tl)),
                                jnp.broadcast_to(x[:, tl:], (half, tl))], axis=0)

    def window(prev_ref, cur_ref, n, cols):
        if n == 0:
            return jnp.concatenate([prev_ref[0, 0, :, cols], cur_ref[0, 0, 0:tl, cols]], axis=0)
        return cur_ref[0, 0, (n - 1) * tl:(n + 1) * tl, cols]

    def rows_of(n):
        return slice(n * tl, (n + 1) * tl)

    def scores(n, hp):
        cols = slice(hp * LANES, (hp + 1) * LANES)
        qp = q_ref[0, 0, rows_of(n), cols]
        zero = jnp.zeros_like(qp)
        qs = jnp.concatenate([jnp.where(low, qp, zero), jnp.where(low, zero, qp)], axis=0)
        return lax.dot_general(window(kp_ref, kc_ref, n, cols), qs, (((1,), (1,)), ((), ())),
                               preferred_element_type=F32)

    def update(s, n, hp, ml_t):
        cols = slice(hp * LANES, (hp + 1) * LANES)
        key = lax.broadcasted_iota(jnp.int32, (2 * tl, 2 * tl), 0)
        qcol = lax.broadcasted_iota(jnp.int32, (2 * tl, 2 * tl), 1)
        dist = tl + jnp.where(qcol >= tl, qcol - tl, qcol) - key
        valid = (dist >= 0) & (dist <= DL_SPAN)
        if n == 0:
            valid = valid & ((key >= tl) | (i > 0))
        s = jnp.where(valid, s, NEG_INF)
        m_new = jnp.max(s, axis=0, keepdims=True)
        if not first:
            h0 = 2 * hp
            m_prev = jnp.concatenate([ml_t[h0:h0 + 1], ml_t[h0 + 1:h0 + 2]], axis=1)
            l_prev = jnp.concatenate([ml_t[DL_HEADS + h0:DL_HEADS + h0 + 1],
                                      ml_t[DL_HEADS + h0 + 1:DL_HEADS + h0 + 2]], axis=1)
            m_new = jnp.maximum(m_prev, m_new)
            alpha = jnp.exp2(m_prev - m_new)
        p = jnp.exp2(s - m_new)
        l_new = jnp.sum(p, axis=0, keepdims=True)
        pv = lax.dot_general(window(vp_ref, vc_ref, n, cols), p.astype(BF16),
                             (((0,), (0,)), ((), ())), preferred_element_type=F32)
        o_t = jnp.concatenate([pv[:half, :tl], pv[half:, tl:]], axis=0)
        if not first:
            l_new = l_new + alpha * l_prev
            o_t = o_t + pair_rows(alpha) * acc_sc[hp, rows_of(n), :].T
        if last:
            o_ref[0, 0, rows_of(n), cols] = (o_t * pair_rows(1.0 / l_new)).T.astype(o_ref.dtype)
        else:
            acc_ref[0, 0, rows_of(n), cols] = o_t.T
        return m_new, l_new

    blocks = [(n, hp) for n in range(n_sub) for hp in range(n_pairs)]
    pending = [scores(*blk) for blk in blocks[:ahead]]
    for idx, (n, hp) in enumerate(blocks):
        if idx + ahead < len(blocks):
            pending.append(scores(*blocks[idx + ahead]))
        if hp == 0:
            ml_t = None if first else ml_sc[rows_of(n), :].T
            ms, ls = [], []
        m_new, l_new = update(pending[idx], n, hp, ml_t)
        pending[idx] = None
        ms += [m_new[:, :tl], m_new[:, tl:]]
        ls += [l_new[:, :tl], l_new[:, tl:]]
        if hp == n_pairs - 1 and not last:
            pad = jnp.zeros((LANES - 2 * DL_HEADS, tl), F32)
            ml_ref[0, 0, rows_of(n), :] = jnp.concatenate(ms + ls + [pad], axis=0).T


def _dilated_stage(q, k, v, dil, state, last, tl=DL_SPAN, max_sub=4, ahead=4):
    B, _, L, D = q.shape
    first = state is None
    n_sub = min(max_sub, L // tl)
    TL = n_sub * tl
    cur = lambda b, r, i: (b, r, i, 0)
    prev = lambda b, r, i: (b, r, jnp.maximum(i * n_sub - 1, 0), 0)
    tile = pl.BlockSpec((1, 1, TL, D), cur)
    halo = pl.BlockSpec((1, 1, tl, D), prev)
    in_specs = [tile, halo, tile, halo, tile]
    args = [q, k, k, v, v]
    scratch = []
    if not first:
        fine = lambda b, r, i: (b, 0, r, i, 0)
        in_specs += [pl.BlockSpec((1, 4, 1, TL // 4, D), fine),
                     pl.BlockSpec((1, 4, 1, TL // 4, LANES), fine)]
        args += [state[0].reshape(B, 4, dil, L // 4, D), state[1].reshape(B, 4, dil, L // 4, LANES)]
        scratch = [pltpu.VMEM((D // LANES, TL, LANES), F32), pltpu.VMEM((TL, LANES), F32)]
    if last:
        out_shape = [jax.ShapeDtypeStruct((B, dil, L, D), BF16)]
        out_specs = [tile]
    else:
        out_shape = [jax.ShapeDtypeStruct((B, dil, L, D), F32),
                     jax.ShapeDtypeStruct((B, dil, L, LANES), F32)]
        out_specs = [tile, pl.BlockSpec((1, 1, TL, LANES), cur)]
    outs = pl.pallas_call(
        functools.partial(_dil_kernel, tl=tl, n_sub=n_sub, ahead=ahead, first=first, last=last),
        grid=(B, dil, L // TL),
        in_specs=in_specs,
        out_specs=out_specs,
        out_shape=out_shape,
        scratch_shapes=scratch,
        compiler_params=_params(("parallel", "parallel", "arbitrary")),
        name=f"dilated_attn_d{dil}",
    )(*args)
    return outs[0] if last else tuple(outs)


def _dilated_attention(qkv_by_dil):
    dils = sorted((d for _, d in DL_PAIRS), reverse=True)
    assert all(w // d == DL_SPAN for w, d in DL_PAIRS)
    assert all(a == 4 * b for a, b in zip(dils, dils[1:])) and dils[-1] == 1
    state = None
    for d in dils:
        state = _dilated_stage(*qkv_by_dil[d], d, state, last=(d == 1))
    return state


def _tail_kernel(*refs, n_mix, ffn_chunk):
    x_ref = refs[0]
    mix_refs = refs[1:1 + n_mix]
    wo_refs = refs[1 + n_mix:1 + 2 * n_mix]
    (g1_ref, b1_ref, w1_ref, w2_ref, g2_ref, b2_ref,
     p_ref, wp_ref, wg_ref, pg_ref, o_ref) = refs[1 + 2 * n_mix:]
    mix = jnp.dot(mix_refs[0][...], wo_refs[0][...], preferred_element_type=F32)
    for m_ref, w_ref in zip(mix_refs[1:], wo_refs[1:]):
        mix = mix + jnp.dot(m_ref[...], w_ref[...], preferred_element_type=F32)
    x1 = _layer_norm(ALPHA * x_ref[...] + mix, g1_ref[...], b1_ref[...])
    x1b = x1.astype(BF16)
    ffn = None
    for c in range(0, FFN_DIM, ffn_chunk):
        hdn = jnp.dot(x1b, w1_ref[:, c:c + ffn_chunk], preferred_element_type=F32)
        hdn = jnp.square(jnp.maximum(hdn, 0.0)).astype(BF16)
        part = jnp.dot(hdn, w2_ref[c:c + ffn_chunk, :], preferred_element_type=F32)
        ffn = part if ffn is None else ffn + part
    x2 = _layer_norm(ALPHA * x1 + ffn, g2_ref[...], b2_ref[...])
    e = _rms_norm(jnp.dot(p_ref[...].astype(BF16), wp_ref[...], preferred_element_type=F32),
                  pg_ref[...])
    gate = _sigmoid(jnp.dot(x2.astype(BF16), wg_ref[...], preferred_element_type=F32))
    o_ref[...] = x2 + gate * e


def _layer_tail(x2, mixes, w_outs, g1, b1, w1, w2, g2, b2, p2, wp, wg, pg, tm=512,
                ffn_chunk=1024):
    T, D = x2.shape
    rows = lambda w: pl.BlockSpec((tm, w), lambda i: (i, 0))
    vec = lambda a: a.reshape(1, -1)
    in_specs = ([rows(D)] + [rows(m.shape[1]) for m in mixes]
                + [_resident(w.shape) for w in w_outs]
                + [_resident((1, D)), _resident((1, D)), _resident(w1.shape),
                   _resident(w2.shape), _resident((1, D)), _resident((1, D)),
                   rows(PLE_DIM), _resident(wp.shape), _resident(wg.shape),
                   _resident((1, D))])
    return pl.pallas_call(
        functools.partial(_tail_kernel, n_mix=len(mixes), ffn_chunk=ffn_chunk),
        grid=(T // tm,),
        in_specs=in_specs,
        out_specs=rows(D),
        out_shape=jax.ShapeDtypeStruct((T, D), F32),
        compiler_params=_params(("parallel",)),
        name="layer_tail",
    )(x2, *mixes, *w_outs, vec(g1), vec(b1), w1, w2, vec(g2), vec(b2), p2, wp, wg, vec(pg))


def kernel(x, p, ev_w_in, ev_w_out, da_lambda, da_subln_g, hg_lb_logits, hg_norm_g,
           od_w_in, od_w_out, ln1_g, ln1_b, ffn_w1, ffn_w2, ln2_g, ln2_b,
           ple_w_proj, ple_w_gate, ple_norm_g):
    B, S, D = x.shape
    T = B * S
    tables = _rope_lane_tables(S)
    lb_all = jnp.cumsum(jax.nn.softmax(hg_lb_logits.astype(F32), axis=0), axis=0)
    bf = lambda w: w.astype(BF16)
    x2 = x.reshape(T, D)
    for l in range(DEPTH):
        j = l // 2
        if l % 2 == 0:
            lam_init = 0.8 - 0.6 * math.exp(-0.3 * l)
            lp = da_lambda[j].astype(F32)
            lam = (jnp.exp(jnp.sum(lp[0] * lp[1])) - jnp.exp(jnp.sum(lp[2] * lp[3]))
                   + lam_init).reshape(1)
            q, k, vt, o_b = _project_layer0(x2, bf(ev_w_in[j]), tables, lb_all[j].reshape(1, -1),
                                            hg_norm_g[j].reshape(1, -1), S)
            o_a = _diff_attention(q.reshape(B, S, -1), k.reshape(B, S, -1), vt,
                                  lam, da_subln_g[j].reshape(1, -1), lam_init)
            mixes = [o_a.reshape(T, -1), o_b.reshape(T, -1)]
            w_o = bf(ev_w_out[j])
            w_outs = [w_o[:DA_WIDTH], w_o[DA_WIDTH:]]
        else:
            segs = ((0, D, "q"), (D, D, "k"), (2 * D, D, "plain"))
            coarse = tuple(sorted(d for _, d in DL_PAIRS if d > 1))
            outs = _project(x2, bf(od_w_in[j]), tables, S, segs, coarse)
            per = 1 + len(coarse)
            qkv = {1: tuple(outs[n * per].reshape(B, 1, S, D) for n in range(3))}
            for m, d in enumerate(coarse):
                qkv[d] = tuple(outs[n * per + 1 + m] for n in range(3))
            o = _dilated_attention(qkv)
            mixes = [o.reshape(T, D)]
            w_outs = [bf(od_w_out[j])]
        x2 = _layer_tail(x2, mixes, w_outs, ln1_g[l], ln1_b[l], bf(ffn_w1[l]), bf(ffn_w2[l]),
                         ln2_g[l], ln2_b[l], p[l].reshape(T, PLE_DIM), bf(ple_w_proj[l]),
                         bf(ple_w_gate[l]), ple_norm_g[l])
    return x2.reshape(B, S, D)
```

```python
import functools
import math

import jax
import jax.numpy as jnp
import numpy as np
from jax import lax
from jax.experimental import pallas as pl
from jax.experimental.pallas import tpu as pltpu

D_MODEL = 1024
DEPTH = 2
PLE_DIM = 256

DA_HEADS = 4
DA_HEAD_DIM = 64
DA_V_DIM = 2 * DA_HEAD_DIM
DA_QK_WIDTH = DA_HEADS * 2 * DA_HEAD_DIM
DA_WIDTH = DA_HEADS * DA_V_DIM
DA_TQ = 512
DA_VT_ROWS = DA_V_DIM + 16
LOG2_E = math.log2(math.e)

HG_HEADS = 4
HG_KDIM = 128
HG_VDIM = 128
HG_KWIDTH = HG_HEADS * HG_KDIM
HG_WIDTH = HG_HEADS * HG_VDIM
HG_CHUNK = 64

DL_HEADS = 16
DL_HEAD_DIM = D_MODEL // DL_HEADS
DL_PAIRS = ((128, 1), (512, 4), (2048, 16))
DL_SPAN = 128

FFN_DIM = 4 * D_MODEL
ROPE_THETA = 500000.0
ROT_DIM = 16
ALPHA = (2 * DEPTH) ** 0.25
LN_EPS = 1e-5

LANES = 128
VMEM_LIMIT_BYTES = 56 * 1024 * 1024

F32 = jnp.float32
BF16 = jnp.bfloat16
NEG_INF = float("-inf")


def _params(sem, vmem=VMEM_LIMIT_BYTES):
    return pltpu.CompilerParams(dimension_semantics=sem, vmem_limit_bytes=vmem)


def _resident(shape):
    nd = len(shape)
    return pl.BlockSpec(shape, lambda *_: (0,) * nd, pipeline_mode=pl.Buffered(1))


def _sigmoid(x):
    return 1.0 / (1.0 + jnp.exp(-x))


def _layer_norm(x, g, b):
    mu = jnp.mean(x, axis=-1, keepdims=True)
    xc = x - mu
    var = jnp.mean(xc * xc, axis=-1, keepdims=True)
    return xc * lax.rsqrt(var + LN_EPS) * g + b


def _rms_norm(x, g):
    return x * lax.rsqrt(jnp.mean(x * x, axis=-1, keepdims=True) + LN_EPS) * g


def _rope_lane_tables(seq):
    half = ROT_DIM // 2
    f32 = np.float32
    inv = f32(ROPE_THETA) ** (-np.arange(0, ROT_DIM, 2, dtype=f32) / f32(ROT_DIM))
    lane = np.arange(LANES) % 64
    inv_lane = np.where(lane < ROT_DIM, inv[lane % half], f32(0.0)).astype(f32)
    ang = np.arange(seq, dtype=f32)[:, None] * inv_lane[None, :]
    cos, sin = np.cos(ang), np.sin(ang)
    s1 = np.where((lane >= half) & (lane < ROT_DIM), sin, f32(0.0))
    s2 = np.where(lane < half, -sin, f32(0.0))
    return tuple(jnp.asarray(t, dtype=F32) for t in (cos, s1, s2))


def _rope(acc, c, s1, s2, scale):
    parts = []
    for g in range(acc.shape[1] // LANES):
        a = acc[:, g * LANES:(g + 1) * LANES]
        r = (a * c + pltpu.roll(a, ROT_DIM // 2, 1) * s1
             + pltpu.roll(a, LANES - ROT_DIM // 2, 1) * s2)
        parts.append(r if scale is None else r * scale)
    return jnp.concatenate(parts, axis=1)


Q_SCALE = 64 ** -0.5 * LOG2_E


def _proj_kernel(x_ref, w_ref, c_ref, s1_ref, s2_ref, *refs, segs, dils, col_chunk, tm):
    n_out = len(segs) * (1 + len(dils))
    out_refs, stage = refs[:n_out], refs[n_out:]
    xb = x_ref[...].astype(BF16)
    c, s1, s2 = c_ref[...], s1_ref[...], s2_ref[...]
    for n, (start, width, kind) in enumerate(segs):
        nat_ref = out_refs[n * (1 + len(dils))]
        dil_refs = out_refs[n * (1 + len(dils)) + 1:(n + 1) * (1 + len(dils))]
        for j in range(0, width, col_chunk):
            acc = jnp.dot(xb, w_ref[:, start + j:start + j + col_chunk],
                          preferred_element_type=F32)
            if kind != "plain":
                acc = _rope(acc, c, s1, s2, {"q": Q_SCALE, "k": None}[kind])
            nat_ref[:, j:j + col_chunk] = acc.astype(nat_ref.dtype)
            for g in range(col_chunk // LANES):
                cols = slice(j + g * LANES, j + (g + 1) * LANES)
                src, d_prev = stage[0], 1
                src[g] = acc[:, g * LANES:(g + 1) * LANES]
                for n_d, (d, ref) in enumerate(zip(dils, dil_refs)):
                    ratio, rows = d // d_prev, tm // d
                    dst = stage[(n_d + 1) % 2] if n_d + 1 < len(dils) else None
                    for r_prev in range(d_prev):
                        for c_new in range(ratio):
                            r = r_prev + d_prev * c_new
                            piece = src[g, pl.ds(r_prev * (tm // d_prev) + c_new, rows,
                                                 stride=ratio), :]
                            ref[0, r, :, cols] = piece.astype(ref.dtype)
                            if dst is not None:
                                dst[g, r * rows:(r + 1) * rows, :] = piece
                    src, d_prev = dst, d


def _project(x2, w_bf, tables, seq, segs, dils, tm=512, col_chunk=512):
    T, D = x2.shape
    N = w_bf.shape[1]
    nsb = seq // tm
    assert dils and all(b % a == 0 for a, b in zip((1,) + tuple(dils), dils))
    tab_spec = pl.BlockSpec((tm, LANES), lambda i: (i % nsb, 0))
    out_shape, out_specs = [], []
    for (_, w, _) in segs:
        out_shape.append(jax.ShapeDtypeStruct((T, w), BF16))
        out_specs.append(pl.BlockSpec((tm, w), lambda i: (i, 0)))
        for d in dils:
            out_shape.append(jax.ShapeDtypeStruct((T // seq, d, seq // d, w), BF16))
            out_specs.append(pl.BlockSpec((1, d, tm // d, w), lambda i: (i // nsb, 0, i % nsb, 0)))
    return pl.pallas_call(
        functools.partial(_proj_kernel, segs=segs, dils=dils, col_chunk=col_chunk, tm=tm),
        grid=(T // tm,),
        in_specs=[pl.BlockSpec((tm, D), lambda i: (i, 0)), _resident((D, N)),
                  tab_spec, tab_spec, tab_spec],
        out_specs=out_specs,
        out_shape=out_shape,
        scratch_shapes=[pltpu.VMEM((col_chunk // LANES, tm, LANES), F32)] * min(len(dils), 2),
        compiler_params=_params(("parallel",)),
        name="in_proj",
    )(x2, w_bf, *tables)


def _diffattn_kernel(lam_ref, q_ref, k_ref, vt_ref, g_ref, o_ref, acc_sc,
                     *, tq, tk, unroll, ahead, lam_init):
    qi = pl.program_id(2)
    n_diag = tq // tk
    q = q_ref[0]
    lane = lax.broadcasted_iota(jnp.int32, (tq, LANES), 1)
    zero = jnp.zeros_like(q)
    qs = jnp.concatenate([jnp.where(lane < DA_HEAD_DIM, q, zero),
                          jnp.where(lane >= DA_HEAD_DIM, q, zero)], axis=0)
    acc_sc[...] = jnp.zeros(acc_sc.shape, F32)

    def scores(blk):
        start = pl.multiple_of(blk * tk, tk)
        return lax.dot_general(k_ref[0, pl.ds(start, tk), :], qs, (((1,), (1,)), ((), ())),
                               preferred_element_type=F32)

    def update(s, blk, m_prev, diag):
        if diag is not None:
            krow = diag * tk + lax.broadcasted_iota(jnp.int32, (tk, 2 * tq), 0)
            col = lax.broadcasted_iota(jnp.int32, (tk, 2 * tq), 1)
            s = jnp.where(krow <= jnp.where(col >= tq, col - tq, col), s, NEG_INF)
        m_new = jnp.maximum(m_prev, jnp.max(s, axis=0, keepdims=True))
        alpha = jnp.exp2(m_prev - m_new)
        p = jnp.exp2(s - m_new).astype(BF16)
        pv = jnp.dot(vt_ref[0, 0, blk], p, preferred_element_type=F32)
        acc_sc[...] = alpha * acc_sc[...] + pv
        return m_new

    def run(first_blk, n_plain, m, with_diag):
        n = n_plain + (n_diag if with_diag else 0)
        pending = [scores(first_blk + u) for u in range(min(ahead, n))]
        for u in range(n):
            if u + ahead < n:
                pending.append(scores(first_blk + u + ahead))
            m = update(pending[u], first_blk + u, m, u - n_plain if u >= n_plain else None)
        return m

    n_before = qi * n_diag
    n_groups = n_before // unroll
    m = jnp.full((1, 2 * tq), NEG_INF, F32)
    m = lax.fori_loop(0, n_groups, lambda j, c: run(j * unroll, unroll, c, False), m)
    step = math.gcd(n_diag, unroll)
    tails = [functools.partial(run, n_groups * unroll, r, with_diag=True)
             for r in range(0, unroll, step)]
    lax.switch((n_before - n_groups * unroll) // step, tails, m)

    acc = acc_sc[0:DA_V_DIM, :] / acc_sc[DA_V_DIM:DA_V_DIM + 1, :]
    o = acc[:, :tq] - lam_ref[0] * acc[:, tq:]
    o = o * lax.rsqrt(jnp.mean(o * o, axis=0, keepdims=True) + LN_EPS)
    o_ref[0] = (o.T * g_ref[...] * (1.0 - lam_init)).astype(o_ref.dtype)


def _diff_attention(q, k, vt, lam, sub_g, lam_init, tq=512, unroll=8, ahead=2):
    B, S, _ = q.shape
    tk = DA_TQ
    return pl.pallas_call(
        functools.partial(_diffattn_kernel, tq=tq, tk=tk, unroll=unroll, ahead=ahead,
                          lam_init=lam_init),
        grid=(B, DA_HEADS, S // tq),
        in_specs=[
            pl.BlockSpec(memory_space=pltpu.SMEM),
            pl.BlockSpec((1, tq, LANES), lambda b, h, i: (b, i, h)),
            pl.BlockSpec((1, S, LANES), lambda b, h, i: (b, 0, h)),
            pl.BlockSpec((1, 1, S // tk, DA_VT_ROWS, tk), lambda b, h, i: (b, h, 0, 0, 0)),
            pl.BlockSpec((1, DA_V_DIM), lambda b, h, i: (0, 0)),
        ],
        out_specs=pl.BlockSpec((1, tq, LANES), lambda b, h, i: (b, i, h)),
        out_shape=jax.ShapeDtypeStruct((B, S, DA_WIDTH), BF16),
        scratch_shapes=[pltpu.VMEM((DA_VT_ROWS, 2 * tq), F32)],
        compiler_params=_params(("parallel", "parallel", "arbitrary")),
        name="diff_attn",
    )(lam, q, k, vt, sub_g)


class _HgrnTile:
    def __init__(self, hq, hf, hi, hg, lb, og):
        self.hq, self.hf, self.hg, self.lb, self.og = hq, hf, hg, lb, og
        self.vb = hi.astype(BF16)
        self.tc = hq.shape[0]
        self.n = self.tc // HG_CHUNK

    def gates(self):
        self.q = self.hq * _sigmoid(self.hq)
        f = self.lb + (1.0 - self.lb) * _sigmoid(self.hf)
        self.kk = 1.0 - f
        self.logf = jnp.log(f)

    def decays(self):
        C, n, tc, kw = HG_CHUNK, self.n, self.tc, HG_KWIDTH
        ri = lax.broadcasted_iota(jnp.int32, (tc, tc), 0)
        ci = lax.broadcasted_iota(jnp.int32, (tc, tc), 1)
        self.tril = (ci <= ri) & (ci >= ri - ri % C)
        hi = self.logf.astype(BF16)
        rest = self.logf - hi.astype(F32)
        mid = rest.astype(BF16)
        lo = (rest - mid.astype(F32)).astype(BF16)
        parts = jnp.dot(self.tril.astype(BF16), jnp.concatenate([hi, mid, lo], axis=1),
                        preferred_element_type=F32)
        b = (parts[:, :kw] + parts[:, kw:2 * kw] + parts[:, 2 * kw:]).reshape(n, C, kw)
        b_last = b[:, C - 1:C, :]
        b_mid = b[:, C // 2 - 1:C // 2, :]
        q3 = self.q.reshape(n, C, kw)
        k3 = self.kk.reshape(n, C, kw)
        flat = lambda t: t.reshape(tc, kw).astype(BF16)
        self.qa = flat(q3 * jnp.exp(b - b_mid))
        self.ka = flat(k3 * jnp.exp(b_mid - b))
        self.q_out = flat(q3 * jnp.exp(b))
        self.k_st = flat(k3 * jnp.exp(b_last - b))
        self.decay = jnp.exp(b_last)

    def intra(self):
        self.o = []
        for h in range(HG_HEADS):
            kc = slice(h * HG_KDIM, (h + 1) * HG_KDIM)
            vc = slice(h * HG_VDIM, (h + 1) * HG_VDIM)
            a = lax.dot_general(self.qa[:, kc], self.ka[:, kc], (((1,), (1,)), ((), ())),
                                preferred_element_type=F32)
            a = jnp.where(self.tril, a, 0.0).astype(BF16)
            self.o.append(jnp.dot(a, self.vb[:, vc], preferred_element_type=F32))

    def recur(self, states):
        C = HG_CHUNK
        chunk = lambda c: slice(c * C, (c + 1) * C)
        kcol = lambda h: slice(h * HG_KDIM, (h + 1) * HG_KDIM)
        vcol = lambda h: slice(h * HG_VDIM, (h + 1) * HG_VDIM)
        upd = [[lax.dot_general(self.vb[chunk(c), vcol(h)], self.k_st[chunk(c), kcol(h)],
                                (((0,), (0,)), ((), ())), preferred_element_type=F32)
                for c in range(self.n)] for h in range(HG_HEADS)]
        before = [[None] * self.n for _ in range(HG_HEADS)]
        for h in range(HG_HEADS):
            st = states[h]
            for c in range(self.n):
                before[h][c] = st.astype(BF16)
                st = self.decay[c, :, kcol(h)] * st + upd[h][c]
            states = states[:h] + [st] + states[h + 1:]
        for h in range(HG_HEADS):
            inter = [lax.dot_general(self.q_out[chunk(c), kcol(h)], before[h][c],
                                     (((1,), (1,)), ((), ())), preferred_element_type=F32)
                     for c in range(self.n)]
            self.o[h] = self.o[h] + jnp.concatenate(inter, axis=0)
        return states

    def finish(self, o_ref, rows):
        for h in range(HG_HEADS):
            vc = slice(h * HG_VDIM, (h + 1) * HG_VDIM)
            hg = self.hg[:, vc]
            o = _rms_norm(self.o[h], self.og) * (hg * _sigmoid(hg))
            o_ref[rows, vc] = o.astype(o_ref.dtype)


def _proj0_kernel(x_ref, w_ref, c_ref, s1_ref, s2_ref, lb_ref, og_ref,
                  q_ref, k_ref, vt_ref, ob_ref, state_sc, *, tm, tc, nsb):
    @pl.when(pl.program_id(0) % nsb == 0)
    def _():
        state_sc[...] = jnp.zeros(state_sc.shape, F32)

    xb = x_ref[...].astype(BF16)
    proj = lambda start, width: jnp.dot(xb, w_ref[:, start:start + width],
                                        preferred_element_type=F32)
    g0 = 2 * DA_QK_WIDTH + DA_WIDTH
    hq = proj(g0, HG_KWIDTH)
    hf = proj(g0 + HG_KWIDTH, HG_KWIDTH)
    hi = proj(g0 + 2 * HG_KWIDTH, HG_WIDTH)
    hg = proj(g0 + 2 * HG_KWIDTH + HG_WIDTH, HG_WIDTH)
    lb, og = lb_ref[...], og_ref[...]
    tiles = [_HgrnTile(hq[r:r + tc], hf[r:r + tc], hi[r:r + tc], hg[r:r + tc], lb, og)
             for r in range(0, tm, tc)]

    cw = 2 * LANES

    def emit_qk(out_ref, start, j, scale):
        acc = _rope(proj(start + j, cw), c_ref[...], s1_ref[...], s2_ref[...], scale)
        out_ref[:, j:j + cw] = acc.astype(out_ref.dtype)

    def emit_vt(j):
        acc = proj(2 * DA_QK_WIDTH + j, cw)
        for g in range(cw // DA_V_DIM):
            head = j // DA_V_DIM + g
            for sb in range(tm // DA_TQ):
                t = acc[sb * DA_TQ:(sb + 1) * DA_TQ, g * DA_V_DIM:(g + 1) * DA_V_DIM].T
                vt_ref[0, head, sb, 0:DA_V_DIM, :] = t.astype(vt_ref.dtype)
                vt_ref[0, head, sb, DA_V_DIM:, :] = jnp.ones(
                    (DA_VT_ROWS - DA_V_DIM, DA_TQ), vt_ref.dtype)

    big = ([functools.partial(emit_qk, q_ref, 0, j, Q_SCALE) for j in range(0, DA_QK_WIDTH, cw)]
           + [functools.partial(emit_qk, k_ref, DA_QK_WIDTH, j, None)
              for j in range(0, DA_QK_WIDTH, cw)]
           + [functools.partial(emit_vt, j) for j in range(0, DA_WIDTH, cw)])
    states = [state_sc[h] for h in range(HG_HEADS)]

    def interleaved(phase):
        if big:
            big.pop(0)()
        return phase()

    for t in tiles:
        t.gates()
    for t in tiles:
        interleaved(t.decays)
    for t in tiles:
        interleaved(t.intra)
    for t in tiles:
        states = interleaved(functools.partial(t.recur, states))
    for n, t in enumerate(tiles):
        interleaved(functools.partial(t.finish, ob_ref, slice(n * tc, (n + 1) * tc)))
    while big:
        big.pop(0)()
    for h in range(HG_HEADS):
        state_sc[h] = states[h]


def _project_layer0(x2, w_bf, tables, lb, out_g, seq, tm=512, tc=256):
    T, D = x2.shape
    N = w_bf.shape[1]
    nsb = seq // tm
    rows = lambda w: pl.BlockSpec((tm, w), lambda i: (i, 0))
    tab_spec = pl.BlockSpec((tm, LANES), lambda i: (i % nsb, 0))
    vt_shape = (T // seq, DA_HEADS, seq // DA_TQ, DA_VT_ROWS, DA_TQ)
    return pl.pallas_call(
        functools.partial(_proj0_kernel, tm=tm, tc=tc, nsb=nsb),
        grid=(T // tm,),
        in_specs=[rows(D), _resident((D, N)), tab_spec, tab_spec, tab_spec,
                  _resident((1, HG_KWIDTH)), _resident((1, HG_VDIM))],
        out_specs=[rows(DA_QK_WIDTH), rows(DA_QK_WIDTH),
                   pl.BlockSpec((1, DA_HEADS, tm // DA_TQ, DA_VT_ROWS, DA_TQ),
                                lambda i: (i // nsb, 0, i % nsb, 0, 0)),
                   rows(HG_WIDTH)],
        out_shape=[jax.ShapeDtypeStruct((T, DA_QK_WIDTH), BF16),
                   jax.ShapeDtypeStruct((T, DA_QK_WIDTH), BF16),
                   jax.ShapeDtypeStruct(vt_shape, BF16),
                   jax.ShapeDtypeStruct((T, HG_WIDTH), BF16)],
        scratch_shapes=[pltpu.VMEM((HG_HEADS, HG_VDIM, HG_KDIM), F32)],
        compiler_params=_params(("arbitrary",)),
        name="in_proj_hgrn2",
    )(x2, w_bf, *tables, lb, out_g)


def _dil_kernel(*refs, tl, n_sub, ahead, first, last):
    q_ref, kp_ref, kc_ref, vp_ref, vc_ref = refs[:5]
    refs = refs[5:]
    if not first:
        accp_ref, mlp_ref = refs[:2]
        refs = refs[2:]
    if last:
        o_ref = refs[0]
        refs = refs[1:]
    else:
        acc_ref, ml_ref = refs[:2]
        refs = refs[2:]
    i = pl.program_id(2)
    n_pairs = DL_HEADS // 2
    half = DL_HEAD_DIM

    if not first:
        acc_sc, ml_sc = refs
        quarter = n_sub * tl // 4
        for c in range(4):
            for hp in range(n_pairs):
                acc_sc[hp, pl.ds(c, quarter, stride=4), :] = (
                    accp_ref[0, c, 0, :, hp * LANES:(hp + 1) * LANES])
            ml_sc[pl.ds(c, quarter, stride=4), :] = mlp_ref[0, c, 0]

    lane = lax.broadcasted_iota(jnp.int32, (tl, LANES), 1)
    low = lane < half

    def pair_rows(x):
        return jnp.concatenate([jnp.broadcast_to(x[:, :tl], (half, tl)),
                                jnp.broadcast_to(x[:, tl:], (half, tl))], axis=0)

    def window(prev_ref, cur_ref, n, cols):
        if n == 0:
            return jnp.concatenate([prev_ref[0, 0, :, cols], cur_ref[0, 0, 0:tl, cols]], axis=0)
        return cur_ref[0, 0, (n - 1) * tl:(n + 1) * tl, cols]

    def rows_of(n):
        return slice(n * tl, (n + 1) * tl)

    def scores(n, hp):
        cols = slice(hp * LANES, (hp + 1) * LANES)
        qp = q_ref[0, 0, rows_of(n), cols]
        zero = jnp.zeros_like(qp)
        qs = jnp.concatenate([jnp.where(low, qp, zero), jnp.where(low, zero, qp)], axis=0)
        return lax.dot_general(window(kp_ref, kc_ref, n, cols), qs, (((1,), (1,)), ((), ())),
                               preferred_element_type=F32)

    def update(s, n, hp, ml_t):
        cols = slice(hp * LANES, (hp + 1) * LANES)
        key = lax.broadcasted_iota(jnp.int32, (2 * tl, 2 * tl), 0)
        qcol = lax.broadcasted_iota(jnp.int32, (2 * tl, 2 * tl), 1)
        dist = tl + jnp.where(qcol >= tl, qcol - tl, qcol) - key
        valid = (dist >= 0) & (dist <= DL_SPAN)
        if n == 0:
            valid = valid & ((key >= tl) | (i > 0))
        s = jnp.where(valid, s, NEG_INF)
        m_new = jnp.max(s, axis=0, keepdims=True)
        if not first:
            h0 = 2 * hp
            m_prev = jnp.concatenate([ml_t[h0:h0 + 1], ml_t[h0 + 1:h0 + 2]], axis=1)
            l_prev = jnp.concatenate([ml_t[DL_HEADS + h0:DL_HEADS + h0 + 1],
                                      ml_t[DL_HEADS + h0 + 1:DL_HEADS + h0 + 2]], axis=1)
            m_new = jnp.maximum(m_prev, m_new)
            alpha = jnp.exp2(m_prev - m_new)
        p = jnp.exp2(s - m_new)
        l_new = jnp.sum(p, axis=0, keepdims=True)
        pv = lax.dot_general(window(vp_ref, vc_ref, n, cols), p.astype(BF16),
                             (((0,), (0,)), ((), ())), preferred_element_type=F32)
        o_t = jnp.concatenate([pv[:half, :tl], pv[half:, tl:]], axis=0)
        if not first:
            l_new = l_new + alpha * l_prev
            o_t = o_t + pair_rows(alpha) * acc_sc[hp, rows_of(n), :].T
        if last:
            o_ref[0, 0, rows_of(n), cols] = (o_t * pair_rows(1.0 / l_new)).T.astype(o_ref.dtype)
        else:
            acc_ref[0, 0, rows_of(n), cols] = o_t.T
        return m_new, l_new

    blocks = [(n, hp) for n in range(n_sub) for hp in range(n_pairs)]
    pending = [scores(*blk) for blk in blocks[:ahead]]
    for idx, (n, hp) in enumerate(blocks):
        if idx + ahead < len(blocks):
            pending.append(scores(*blocks[idx + ahead]))
        if hp == 0:
            ml_t = None if first else ml_sc[rows_of(n), :].T
            ms, ls = [], []
        m_new, l_new = update(pending[idx], n, hp, ml_t)
        pending[idx] = None
        ms += [m_new[:, :tl], m_new[:, tl:]]
        ls += [l_new[:, :tl], l_new[:, tl:]]
        if hp == n_pairs - 1 and not last:
            pad = jnp.zeros((LANES - 2 * DL_HEADS, tl), F32)
            ml_ref[0, 0, rows_of(n), :] = jnp.concatenate(ms + ls + [pad], axis=0).T


def _dilated_stage(q, k, v, dil, state, last, tl=DL_SPAN, max_sub=4, ahead=4):
    B, _, L, D = q.shape
    first = state is None
    n_sub = min(max_sub, L // tl)
    TL = n_sub * tl
    cur = lambda b, r, i: (b, r, i, 0)
    prev = lambda b, r, i: (b, r, jnp.maximum(i * n_sub - 1, 0), 0)
    tile = pl.BlockSpec((1, 1, TL, D), cur)
    halo = pl.BlockSpec((1, 1, tl, D), prev)
    in_specs = [tile, halo, tile, halo, tile]
    args = [q, k, k, v, v]
    scratch = []
    if not first:
        fine = lambda b, r, i: (b, 0, r, i, 0)
        in_specs += [pl.BlockSpec((1, 4, 1, TL // 4, D), fine),
                     pl.BlockSpec((1, 4, 1, TL // 4, LANES), fine)]
        args += [state[0].reshape(B, 4, dil, L // 4, D), state[1].reshape(B, 4, dil, L // 4, LANES)]
        scratch = [pltpu.VMEM((D // LANES, TL, LANES), F32), pltpu.VMEM((TL, LANES), F32)]
    if last:
        out_shape = [jax.ShapeDtypeStruct((B, dil, L, D), BF16)]
        out_specs = [tile]
    else:
        out_shape = [jax.ShapeDtypeStruct((B, dil, L, D), F32),
                     jax.ShapeDtypeStruct((B, dil, L, LANES), F32)]
        out_specs = [tile, pl.BlockSpec((1, 1, TL, LANES), cur)]
    outs = pl.pallas_call(
        functools.partial(_dil_kernel, tl=tl, n_sub=n_sub, ahead=ahead, first=first, last=last),
        grid=(B, dil, L // TL),
        in_specs=in_specs,
        out_specs=out_specs,
        out_shape=out_shape,
        scratch_shapes=scratch,
        compiler_params=_params(("parallel", "parallel", "arbitrary")),
        name=f"dilated_attn_d{dil}",
    )(*args)
    return outs[0] if last else tuple(outs)


def _dilated_attention(qkv_by_dil):
    dils = sorted((d for _, d in DL_PAIRS), reverse=True)
    assert all(w // d == DL_SPAN for w, d in DL_PAIRS)
    assert all(a == 4 * b for a, b in zip(dils, dils[1:])) and dils[-1] == 1
    state = None
    for d in dils:
        state = _dilated_stage(*qkv_by_dil[d], d, state, last=(d == 1))
    return state


def _tail_kernel(*refs, n_mix, ffn_chunk):
    x_ref = refs[0]
    mix_refs = refs[1:1 + n_mix]
    wo_refs = refs[1 + n_mix:1 + 2 * n_mix]
    (g1_ref, b1_ref, w1_ref, w2_ref, g2_ref, b2_ref,
     p_ref, wp_ref, wg_ref, pg_ref, o_ref) = refs[1 + 2 * n_mix:]
    mix = jnp.dot(mix_refs[0][...], wo_refs[0][...], preferred_element_type=F32)
    for m_ref, w_ref in zip(mix_refs[1:], wo_refs[1:]):
        mix = mix + jnp.dot(m_ref[...], w_ref[...], preferred_element_type=F32)
    x1 = _layer_norm(ALPHA * x_ref[...] + mix, g1_ref[...], b1_ref[...])
    x1b = x1.astype(BF16)
    ffn = None
    for c in range(0, FFN_DIM, ffn_chunk):
        hdn = jnp.dot(x1b, w1_ref[:, c:c + ffn_chunk], preferred_element_type=F32)
        hdn = jnp.square(jnp.maximum(hdn, 0.0)).astype(BF16)
        part = jnp.dot(hdn, w2_ref[c:c + ffn_chunk, :], preferred_element_type=F32)
        ffn = part if ffn is None else ffn + part
    x2 = _layer_norm(ALPHA * x1 + ffn, g2_ref[...], b2_ref[...])
    e = _rms_norm(jnp.dot(p_ref[...].astype(BF16), wp_ref[...], preferred_element_type=F32),
                  pg_ref[...])
    gate = _sigmoid(jnp.dot(x2.astype(BF16), wg_ref[...], preferred_element_type=F32))
    o_ref[...] = x2 + gate * e


def _layer_tail(x2, mixes, w_outs, g1, b1, w1, w2, g2, b2, p2, wp, wg, pg, tm=512,
                ffn_chunk=1024):
    T, D = x2.shape
    rows = lambda w: pl.BlockSpec((tm, w), lambda i: (i, 0))
    vec = lambda a: a.reshape(1, -1)
    in_specs = ([rows(D)] + [rows(m.shape[1]) for m in mixes]
                + [_resident(w.shape) for w in w_outs]
                + [_resident((1, D)), _resident((1, D)), _resident(w1.shape),
                   _resident(w2.shape), _resident((1, D)), _resident((1, D)),
                   rows(PLE_DIM), _resident(wp.shape), _resident(wg.shape),
                   _resident((1, D))])
    return pl.pallas_call(
        functools.partial(_tail_kernel, n_mix=len(mixes), ffn_chunk=ffn_chunk),
        grid=(T // tm,),
        in_specs=in_specs,
        out_specs=rows(D),
        out_shape=jax.ShapeDtypeStruct((T, D), F32),
        compiler_params=_params(("parallel",)),
        name="layer_tail",
    )(x2, *mixes, *w_outs, vec(g1), vec(b1), w1, w2, vec(g2), vec(b2), p2, wp, wg, vec(pg))


def kernel(x, p, ev_w_in, ev_w_out, da_lambda, da_subln_g, hg_lb_logits, hg_norm_g,
           od_w_in, od_w_out, ln1_g, ln1_b, ffn_w1, ffn_w2, ln2_g, ln2_b,
           ple_w_proj, ple_w_gate, ple_norm_g):
    B, S, D = x.shape
    T = B * S
    tables = _rope_lane_tables(S)
    lb_all = jnp.cumsum(jax.nn.softmax(hg_lb_logits.astype(F32), axis=0), axis=0)
    bf = lambda w: w.astype(BF16)
    x2 = x.reshape(T, D)
    for l in range(DEPTH):
        j = l // 2
        if l % 2 == 0:
            lam_init = 0.8 - 0.6 * math.exp(-0.3 * l)
            lp = da_lambda[j].astype(F32)
            lam = (jnp.exp(jnp.sum(lp[0] * lp[1])) - jnp.exp(jnp.sum(lp[2] * lp[3]))
                   + lam_init).reshape(1)
            q, k, vt, o_b = _project_layer0(x2, bf(ev_w_in[j]), tables, lb_all[j].reshape(1, -1),
                                            hg_norm_g[j].reshape(1, -1), S)
            o_a = _diff_attention(q.reshape(B, S, -1), k.reshape(B, S, -1), vt,
                                  lam, da_subln_g[j].reshape(1, -1), lam_init)
            mixes = [o_a.reshape(T, -1), o_b.reshape(T, -1)]
            w_o = bf(ev_w_out[j])
            w_outs = [w_o[:DA_WIDTH], w_o[DA_WIDTH:]]
        else:
            segs = ((0, D, "q"), (D, D, "k"), (2 * D, D, "plain"))
            coarse = tuple(sorted(d for _, d in DL_PAIRS if d > 1))
            outs = _project(x2, bf(od_w_in[j]), tables, S, segs, coarse)
            per = 1 + len(coarse)
            qkv = {1: tuple(outs[n * per].reshape(B, 1, S, D) for n in range(3))}
            for m, d in enumerate(coarse):
                qkv[d] = tuple(outs[n * per + 1 + m] for n in range(3))
            o = _dilated_attention(qkv)
            mixes = [o.reshape(T, D)]
            w_outs = [bf(od_w_out[j])]
        x2 = _layer_tail(x2, mixes, w_outs, ln1_g[l], ln1_b[l], bf(ffn_w1[l]), bf(ffn_w2[l]),
                         ln2_g[l], ln2_b[l], p[l].reshape(T, PLE_DIM), bf(ple_w_proj[l]),
                         bf(ple_w_gate[l]), ple_norm_g[l])
    return x2.reshape(B, S, D)
```

```python
import functools
import math

import jax
import jax.numpy as jnp
import numpy as np
from jax import lax
from jax.experimental import pallas as pl
from jax.experimental.pallas import tpu as pltpu

D_MODEL = 1024
DEPTH = 2
PLE_DIM = 256

DA_HEADS = 4
DA_HEAD_DIM = 64
DA_V_DIM = 2 * DA_HEAD_DIM
DA_QK_WIDTH = DA_HEADS * 2 * DA_HEAD_DIM
DA_WIDTH = DA_HEADS * DA_V_DIM
DA_TQ = 512
DA_VT_ROWS = DA_V_DIM + 16
LOG2_E = math.log2(math.e)

HG_HEADS = 4
HG_KDIM = 128
HG_VDIM = 128
HG_KWIDTH = HG_HEADS * HG_KDIM
HG_WIDTH = HG_HEADS * HG_VDIM
HG_CHUNK = 64

DL_HEADS = 16
DL_HEAD_DIM = D_MODEL // DL_HEADS
DL_PAIRS = ((128, 1), (512, 4), (2048, 16))
DL_SPAN = 128

FFN_DIM = 4 * D_MODEL
ROPE_THETA = 500000.0
ROT_DIM = 16
ALPHA = (2 * DEPTH) ** 0.25
LN_EPS = 1e-5

LANES = 128
VMEM_LIMIT_BYTES = 56 * 1024 * 1024

F32 = jnp.float32
BF16 = jnp.bfloat16
NEG_INF = float("-inf")


def _params(sem, vmem=VMEM_LIMIT_BYTES):
    return pltpu.CompilerParams(dimension_semantics=sem, vmem_limit_bytes=vmem)


def _resident(shape):
    nd = len(shape)
    return pl.BlockSpec(shape, lambda *_: (0,) * nd, pipeline_mode=pl.Buffered(1))


def _sigmoid(x):
    return 1.0 / (1.0 + jnp.exp(-x))


def _layer_norm(x, g, b):
    mu = jnp.mean(x, axis=-1, keepdims=True)
    xc = x - mu
    var = jnp.mean(xc * xc, axis=-1, keepdims=True)
    return xc * lax.rsqrt(var + LN_EPS) * g + b


def _rms_norm(x, g):
    return x * lax.rsqrt(jnp.mean(x * x, axis=-1, keepdims=True) + LN_EPS) * g


def _rope_lane_tables(seq):
    half = ROT_DIM // 2
    f32 = np.float32
    inv = f32(ROPE_THETA) ** (-np.arange(0, ROT_DIM, 2, dtype=f32) / f32(ROT_DIM))
    lane = np.arange(LANES) % 64
    inv_lane = np.where(lane < ROT_DIM, inv[lane % half], f32(0.0)).astype(f32)
    ang = np.arange(seq, dtype=f32)[:, None] * inv_lane[None, :]
    cos, sin = np.cos(ang), np.sin(ang)
    s1 = np.where((lane >= half) & (lane < ROT_DIM), sin, f32(0.0))
    s2 = np.where(lane < half, -sin, f32(0.0))
    return tuple(jnp.asarray(t, dtype=F32) for t in (cos, s1, s2))


def _rope(acc, c, s1, s2, scale):
    parts = []
    for g in range(acc.shape[1] // LANES):
        a = acc[:, g * LANES:(g + 1) * LANES]
        r = (a * c + pltpu.roll(a, ROT_DIM // 2, 1) * s1
             + pltpu.roll(a, LANES - ROT_DIM // 2, 1) * s2)
        parts.append(r if scale is None else r * scale)
    return jnp.concatenate(parts, axis=1)


Q_SCALE = 64 ** -0.5 * LOG2_E


def _proj_kernel(x_ref, w_ref, c_ref, s1_ref, s2_ref, *refs, segs, dils, col_chunk, tm):
    n_out = len(segs) * (1 + len(dils))
    out_refs, stage = refs[:n_out], refs[n_out:]
    xb = x_ref[...].astype(BF16)
    c, s1, s2 = c_ref[...], s1_ref[...], s2_ref[...]
    for n, (start, width, kind) in enumerate(segs):
        nat_ref = out_refs[n * (1 + len(dils))]
        dil_refs = out_refs[n * (1 + len(dils)) + 1:(n + 1) * (1 + len(dils))]
        for j in range(0, width, col_chunk):
            acc = jnp.dot(xb, w_ref[:, start + j:start + j + col_chunk],
                          preferred_element_type=F32)
            if kind != "plain":
                acc = _rope(acc, c, s1, s2, {"q": Q_SCALE, "k": None}[kind])
            nat_ref[:, j:j + col_chunk] = acc.astype(nat_ref.dtype)
            for g in range(col_chunk // LANES):
                cols = slice(j + g * LANES, j + (g + 1) * LANES)
                src, d_prev = stage[0], 1
                src[g] = acc[:, g * LANES:(g + 1) * LANES]
                for n_d, (d, ref) in enumerate(zip(dils, dil_refs)):
                    ratio, rows = d // d_prev, tm // d
                    dst = stage[(n_d + 1) % 2] if n_d + 1 < len(dils) else None
                    for r_prev in range(d_prev):
                        for c_new in range(ratio):
                            r = r_prev + d_prev * c_new
                            piece = src[g, pl.ds(r_prev * (tm // d_prev) + c_new, rows,
                                                 stride=ratio), :]
                            ref[0, r, :, cols] = piece.astype(ref.dtype)
                            if dst is not None:
                                dst[g, r * rows:(r + 1) * rows, :] = piece
                    src, d_prev = dst, d


def _project(x2, w_bf, tables, seq, segs, dils, tm=512, col_chunk=512):
    T, D = x2.shape
    N = w_bf.shape[1]
    nsb = seq // tm
    assert dils and all(b % a == 0 for a, b in zip((1,) + tuple(dils), dils))
    tab_spec = pl.BlockSpec((tm, LANES), lambda i: (i % nsb, 0))
    out_shape, out_specs = [], []
    for (_, w, _) in segs:
        out_shape.append(jax.ShapeDtypeStruct((T, w), BF16))
        out_specs.append(pl.BlockSpec((tm, w), lambda i: (i, 0)))
        for d in dils:
            out_shape.append(jax.ShapeDtypeStruct((T // seq, d, seq // d, w), BF16))
            out_specs.append(pl.BlockSpec((1, d, tm // d, w), lambda i: (i // nsb, 0, i % nsb, 0)))
    return pl.pallas_call(
        functools.partial(_proj_kernel, segs=segs, dils=dils, col_chunk=col_chunk, tm=tm),
        grid=(T // tm,),
        in_specs=[pl.BlockSpec((tm, D), lambda i: (i, 0)), _resident((D, N)),
                  tab_spec, tab_spec, tab_spec],
        out_specs=out_specs,
        out_shape=out_shape,
        scratch_shapes=[pltpu.VMEM((col_chunk // LANES, tm, LANES), F32)] * min(len(dils), 2),
        compiler_params=_params(("parallel",)),
        name="in_proj",
    )(x2, w_bf, *tables)


def _diffattn_kernel(lam_ref, q_ref, k_ref, vt_ref, g_ref, o_ref, acc_sc,
                     *, tq, tk, unroll, ahead, lam_init):
    qi = pl.program_id(2)
    n_diag = tq // tk
    q = q_ref[0]
    lane = lax.broadcasted_iota(jnp.int32, (tq, LANES), 1)
    zero = jnp.zeros_like(q)
    qs = jnp.concatenate([jnp.where(lane < DA_HEAD_DIM, q, zero),
                          jnp.where(lane >= DA_HEAD_DIM, q, zero)], axis=0)
    acc_sc[...] = jnp.zeros(acc_sc.shape, F32)

    def scores(blk):
        start = pl.multiple_of(blk * tk, tk)
        return lax.dot_general(k_ref[0, pl.ds(start, tk), :], qs, (((1,), (1,)), ((), ())),
                               preferred_element_type=F32)

    def update(s, blk, m_prev, diag):
        if diag is not None:
            krow = diag * tk + lax.broadcasted_iota(jnp.int32, (tk, 2 * tq), 0)
            col = lax.broadcasted_iota(jnp.int32, (tk, 2 * tq), 1)
            s = jnp.where(krow <= jnp.where(col >= tq, col - tq, col), s, NEG_INF)
        m_new = jnp.maximum(m_prev, jnp.max(s, axis=0, keepdims=True))
        alpha = jnp.exp2(m_prev - m_new)
        p = jnp.exp2(s - m_new).astype(BF16)
        pv = jnp.dot(vt_ref[0, 0, blk], p, preferred_element_type=F32)
        acc_sc[...] = alpha * acc_sc[...] + pv
        return m_new

    def run(first_blk, n_plain, m, with_diag):
        n = n_plain + (n_diag if with_diag else 0)
        pending = [scores(first_blk + u) for u in range(min(ahead, n))]
        for u in range(n):
            if u + ahead < n:
                pending.append(scores(first_blk + u + ahead))
            m = update(pending[u], first_blk + u, m, u - n_plain if u >= n_plain else None)
        return m

    n_before = qi * n_diag
    n_groups = n_before // unroll
    m = jnp.full((1, 2 * tq), NEG_INF, F32)
    m = lax.fori_loop(0, n_groups, lambda j, c: run(j * unroll, unroll, c, False), m)
    step = math.gcd(n_diag, unroll)
    tails = [functools.partial(run, n_groups * unroll, r, with_diag=True)
             for r in range(0, unroll, step)]
    lax.switch((n_before - n_groups * unroll) // step, tails, m)

    acc = acc_sc[0:DA_V_DIM, :] / acc_sc[DA_V_DIM:DA_V_DIM + 1, :]
    o = acc[:, :tq] - lam_ref[0] * acc[:, tq:]
    o = o * lax.rsqrt(jnp.mean(o * o, axis=0, keepdims=True) + LN_EPS)
    o_ref[0] = (o.T * g_ref[...] * (1.0 - lam_init)).astype(o_ref.dtype)


def _diff_attention(q, k, vt, lam, sub_g, lam_init, tq=512, unroll=8, ahead=2):
    B, S, _ = q.shape
    tk = DA_TQ
    return pl.pallas_call(
        functools.partial(_diffattn_kernel, tq=tq, tk=tk, unroll=unroll, ahead=ahead,
                          lam_init=lam_init),
        grid=(B, DA_HEADS, S // tq),
        in_specs=[
            pl.BlockSpec(memory_space=pltpu.SMEM),
            pl.BlockSpec((1, tq, LANES), lambda b, h, i: (b, i, h)),
            pl.BlockSpec((1, S, LANES), lambda b, h, i: (b, 0, h)),
            pl.BlockSpec((1, 1, S // tk, DA_VT_ROWS, tk), lambda b, h, i: (b, h, 0, 0, 0)),
            pl.BlockSpec((1, DA_V_DIM), lambda b, h, i: (0, 0)),
        ],
        out_specs=pl.BlockSpec((1, tq, LANES), lambda b, h, i: (b, i, h)),
        out_shape=jax.ShapeDtypeStruct((B, S, DA_WIDTH), BF16),
        scratch_shapes=[pltpu.VMEM((DA_VT_ROWS, 2 * tq), F32)],
        compiler_params=_params(("parallel", "parallel", "arbitrary")),
        name="diff_attn",
    )(lam, q, k, vt, sub_g)


class _HgrnTile:
    def __init__(self, hq, hf, hi, hg, lb, og):
        self.hq, self.hf, self.hg, self.lb, self.og = hq, hf, hg, lb, og
        self.vb = hi.astype(BF16)
        self.tc = hq.shape[0]
        self.n = self.tc // HG_CHUNK

    def gates(self):
        self.q = self.hq * _sigmoid(self.hq)
        f = self.lb + (1.0 - self.lb) * _sigmoid(self.hf)
        self.kk = 1.0 - f
        self.logf = jnp.log(f)

    def decays(self):
        C, n, tc, kw = HG_CHUNK, self.n, self.tc, HG_KWIDTH
        ri = lax.broadcasted_iota(jnp.int32, (tc, tc), 0)
        ci = lax.broadcasted_iota(jnp.int32, (tc, tc), 1)
        self.tril = (ci <= ri) & (ci >= ri - ri % C)
        hi = self.logf.astype(BF16)
        rest = self.logf - hi.astype(F32)
        mid = rest.astype(BF16)
        lo = (rest - mid.astype(F32)).astype(BF16)
        parts = jnp.dot(self.tril.astype(BF16), jnp.concatenate([hi, mid, lo], axis=1),
                        preferred_element_type=F32)
        b = (parts[:, :kw] + parts[:, kw:2 * kw] + parts[:, 2 * kw:]).reshape(n, C, kw)
        b_last = b[:, C - 1:C, :]
        b_mid = b[:, C // 2 - 1:C // 2, :]
        q3 = self.q.reshape(n, C, kw)
        k3 = self.kk.reshape(n, C, kw)
        flat = lambda t: t.reshape(tc, kw).astype(BF16)
        self.qa = flat(q3 * jnp.exp(b - b_mid))
        self.ka = flat(k3 * jnp.exp(b_mid - b))
        self.q_out = flat(q3 * jnp.exp(b))
        self.k_st = flat(k3 * jnp.exp(b_last - b))
        self.decay = jnp.exp(b_last)

    def intra(self):
        self.o = []
        for h in range(HG_HEADS):
            kc = slice(h * HG_KDIM, (h + 1) * HG_KDIM)
            vc = slice(h * HG_VDIM, (h + 1) * HG_VDIM)
            a = lax.dot_general(self.qa[:, kc], self.ka[:, kc], (((1,), (1,)), ((), ())),
                                preferred_element_type=F32)
            a = jnp.where(self.tril, a, 0.0).astype(BF16)
            self.o.append(jnp.dot(a, self.vb[:, vc], preferred_element_type=F32))

    def recur(self, states):
        C = HG_CHUNK
        chunk = lambda c: slice(c * C, (c + 1) * C)
        kcol = lambda h: slice(h * HG_KDIM, (h + 1) * HG_KDIM)
        vcol = lambda h: slice(h * HG_VDIM, (h + 1) * HG_VDIM)
        upd = [[lax.dot_general(self.vb[chunk(c), vcol(h)], self.k_st[chunk(c), kcol(h)],
                                (((0,), (0,)), ((), ())), preferred_element_type=F32)
                for c in range(self.n)] for h in range(HG_HEADS)]
        before = [[None] * self.n for _ in range(HG_HEADS)]
        for h in range(HG_HEADS):
            st = states[h]
            for c in range(self.n):
                before[h][c] = st.astype(BF16)
                st = self.decay[c, :, kcol(h)] * st + upd[h][c]
            states = states[:h] + [st] + states[h + 1:]
        for h in range(HG_HEADS):
            inter = [lax.dot_general(self.q_out[chunk(c), kcol(h)], before[h][c],
                                     (((1,), (1,)), ((), ())), preferred_element_type=F32)
                     for c in range(self.n)]
            self.o[h] = self.o[h] + jnp.concatenate(inter, axis=0)
        return states

    def finish(self, o_ref, rows):
        for h in range(HG_HEADS):
            vc = slice(h * HG_VDIM, (h + 1) * HG_VDIM)
            hg = self.hg[:, vc]
            o = _rms_norm(self.o[h], self.og) * (hg * _sigmoid(hg))
            o_ref[rows, vc] = o.astype(o_ref.dtype)


def _proj0_kernel(x_ref, w_ref, c_ref, s1_ref, s2_ref, lb_ref, og_ref,
                  q_ref, k_ref, vt_ref, ob_ref, state_sc, *, tm, tc, nsb):
    @pl.when(pl.program_id(0) % nsb == 0)
    def _():
        state_sc[...] = jnp.zeros(state_sc.shape, F32)

    xb = x_ref[...].astype(BF16)
    proj = lambda start, width: jnp.dot(xb, w_ref[:, start:start + width],
                                        preferred_element_type=F32)
    g0 = 2 * DA_QK_WIDTH + DA_WIDTH
    hq = proj(g0, HG_KWIDTH)
    hf = proj(g0 + HG_KWIDTH, HG_KWIDTH)
    hi = proj(g0 + 2 * HG_KWIDTH, HG_WIDTH)
    hg = proj(g0 + 2 * HG_KWIDTH + HG_WIDTH, HG_WIDTH)
    lb, og = lb_ref[...], og_ref[...]
    tiles = [_HgrnTile(hq[r:r + tc], hf[r:r + tc], hi[r:r + tc], hg[r:r + tc], lb, og)
             for r in range(0, tm, tc)]

    cw = 2 * LANES

    def emit_qk(out_ref, start, j, scale):
        acc = _rope(proj(start + j, cw), c_ref[...], s1_ref[...], s2_ref[...], scale)
        out_ref[:, j:j + cw] = acc.astype(out_ref.dtype)

    def emit_vt(j):
        acc = proj(2 * DA_QK_WIDTH + j, cw)
        for g in range(cw // DA_V_DIM):
            head = j // DA_V_DIM + g
            for sb in range(tm // DA_TQ):
                t = acc[sb * DA_TQ:(sb + 1) * DA_TQ, g * DA_V_DIM:(g + 1) * DA_V_DIM].T
                vt_ref[0, head, sb, 0:DA_V_DIM, :] = t.astype(vt_ref.dtype)
                vt_ref[0, head, sb, DA_V_DIM:, :] = jnp.ones(
                    (DA_VT_ROWS - DA_V_DIM, DA_TQ), vt_ref.dtype)

    big = ([functools.partial(emit_qk, q_ref, 0, j, Q_SCALE) for j in range(0, DA_QK_WIDTH, cw)]
           + [functools.partial(emit_qk, k_ref, DA_QK_WIDTH, j, None)
              for j in range(0, DA_QK_WIDTH, cw)]
           + [functools.partial(emit_vt, j) for j in range(0, DA_WIDTH, cw)])
    states = [state_sc[h] for h in range(HG_HEADS)]

    def interleaved(phase):
        if big:
            big.pop(0)()
        return phase()

    for t in tiles:
        t.gates()
    for t in tiles:
        interleaved(t.decays)
    for t in tiles:
        interleaved(t.intra)
    for t in tiles:
        states = interleaved(functools.partial(t.recur, states))
    for n, t in enumerate(tiles):
        interleaved(functools.partial(t.finish, ob_ref, slice(n * tc, (n + 1) * tc)))
    while big:
        big.pop(0)()
    for h in range(HG_HEADS):
        state_sc[h] = states[h]


def _project_layer0(x2, w_bf, tables, lb, out_g, seq, tm=512, tc=256):
    T, D = x2.shape
    N = w_bf.shape[1]
    nsb = seq // tm
    rows = lambda w: pl.BlockSpec((tm, w), lambda i: (i, 0))
    tab_spec = pl.BlockSpec((tm, LANES), lambda i: (i % nsb, 0))
    vt_shape = (T // seq, DA_HEADS, seq // DA_TQ, DA_VT_ROWS, DA_TQ)
    return pl.pallas_call(
        functools.partial(_proj0_kernel, tm=tm, tc=tc, nsb=nsb),
        grid=(T // tm,),
        in_specs=[rows(D), _resident((D, N)), tab_spec, tab_spec, tab_spec,
                  _resident((1, HG_KWIDTH)), _resident((1, HG_VDIM))],
        out_specs=[rows(DA_QK_WIDTH), rows(DA_QK_WIDTH),
                   pl.BlockSpec((1, DA_HEADS, tm // DA_TQ, DA_VT_ROWS, DA_TQ),
                                lambda i: (i // nsb, 0, i % nsb, 0, 0)),
                   rows(HG_WIDTH)],
        out_shape=[jax.ShapeDtypeStruct((T, DA_QK_WIDTH), BF16),
                   jax.ShapeDtypeStruct((T, DA_QK_WIDTH), BF16),
                   jax.ShapeDtypeStruct(vt_shape, BF16),
                   jax.ShapeDtypeStruct((T, HG_WIDTH), BF16)],
        scratch_shapes=[pltpu.VMEM((HG_HEADS, HG_VDIM, HG_KDIM), F32)],
        compiler_params=_params(("arbitrary",)),
        name="in_proj_hgrn2",
    )(x2, w_bf, *tables, lb, out_g)


def _dil_kernel(*refs, tl, n_sub, ahead, first, last):
    q_ref, kp_ref, kc_ref, vp_ref, vc_ref = refs[:5]
    refs = refs[5:]
    if not first:
        accp_ref, mlp_ref = refs[:2]
        refs = refs[2:]
    if last:
        o_ref = refs[0]
        refs = refs[1:]
    else:
        acc_ref, ml_ref = refs[:2]
        refs = refs[2:]
    i = pl.program_id(2)
    n_pairs = DL_HEADS // 2
    half = DL_HEAD_DIM

    if not first:
        acc_sc, ml_sc = refs
        quarter = n_sub * tl // 4
        for c in range(4):
            for hp in range(n_pairs):
                acc_sc[hp, pl.ds(c, quarter, stride=4), :] = (
                    accp_ref[0, c, 0, :, hp * LANES:(hp + 1) * LANES])
            ml_sc[pl.ds(c, quarter, stride=4), :] = mlp_ref[0, c, 0]

    lane = lax.broadcasted_iota(jnp.int32, (tl, LANES), 1)
    low = lane < half

    def pair_rows(x):
        return jnp.concatenate([jnp.broadcast_to(x[:, :tl], (half, tl)),
                                jnp.broadcast_to(x[:, tl:], (half, tl))], axis=0)

    def window(prev_ref, cur_ref, n, cols):
        if n == 0:
            return jnp.concatenate([prev_ref[0, 0, :, cols], cur_ref[0, 0, 0:tl, cols]], axis=0)
        return cur_ref[0, 0, (n - 1) * tl:(n + 1) * tl, cols]

    def rows_of(n):
        return slice(n * tl, (n + 1) * tl)

    def scores(n, hp):
        cols = slice(hp * LANES, (hp + 1) * LANES)
        qp = q_ref[0, 0, rows_of(n), cols]
        zero = jnp.zeros_like(qp)
        qs = jnp.concatenate([jnp.where(low, qp, zero), jnp.where(low, zero, qp)], axis=0)
        return lax.dot_general(window(kp_ref, kc_ref, n, cols), qs, (((1,), (1,)), ((), ())),
                               preferred_element_type=F32)

    def update(s, n, hp, ml_t):
        cols = slice(hp * LANES, (hp + 1) * LANES)
        key = lax.broadcasted_iota(jnp.int32, (2 * tl, 2 * tl), 0)
        qcol = lax.broadcasted_iota(jnp.int32, (2 * tl, 2 * tl), 1)
        dist = tl + jnp.where(qcol >= tl, qcol - tl, qcol) - key
        valid = (dist >= 0) & (dist <= DL_SPAN)
        if n == 0:
            valid = valid & ((key >= tl) | (i > 0))
        s = jnp.where(valid, s, NEG_INF)
        m_new = jnp.max(s, axis=0, keepdims=True)
        if not first:
            h0 = 2 * hp
            m_prev = jnp.concatenate([ml_t[h0:h0 + 1], ml_t[h0 + 1:h0 + 2]], axis=1)
            l_prev = jnp.concatenate([ml_t[DL_HEADS + h0:DL_HEADS + h0 + 1],
                                      ml_t[DL_HEADS + h0 + 1:DL_HEADS + h0 + 2]], axis=1)
            m_new = jnp.maximum(m_prev, m_new)
            alpha = jnp.exp2(m_prev - m_new)
        p = jnp.exp2(s - m_new)
        l_new = jnp.sum(p, axis=0, keepdims=True)
        pv = lax.dot_general(window(vp_ref, vc_ref, n, cols), p.astype(BF16),
                             (((0,), (0,)), ((), ())), preferred_element_type=F32)
        o_t = jnp.concatenate([pv[:half, :tl], pv[half:, tl:]], axis=0)
        if not first:
            l_new = l_new + alpha * l_prev
            o_t = o_t + pair_rows(alpha) * acc_sc[hp, rows_of(n), :].T
        if last:
            o_ref[0, 0, rows_of(n), cols] = (o_t * pair_rows(1.0 / l_new)).T.astype(o_ref.dtype)
        else:
            acc_ref[0, 0, rows_of(n), cols] = o_t.T
        return m_new, l_new

    blocks = [(n, hp) for n in range(n_sub) for hp in range(n_pairs)]
    pending = [scores(*blk) for blk in blocks[:ahead]]
    for idx, (n, hp) in enumerate(blocks):
        if idx + ahead < len(blocks):
            pending.append(scores(*blocks[idx + ahead]))
        if hp == 0:
            ml_t = None if first else ml_sc[rows_of(n), :].T
            ms, ls = [], []
        m_new, l_new = update(pending[idx], n, hp, ml_t)
        pending[idx] = None
        ms += [m_new[:, :tl], m_new[:, tl:]]
        ls += [l_new[:, :tl], l_new[:, tl:]]
        if hp == n_pairs - 1 and not last:
            pad = jnp.zeros((LANES - 2 * DL_HEADS, tl), F32)
            ml_ref[0, 0, rows_of(n), :] = jnp.concatenate(ms + ls + [pad], axis=0).T


def _dilated_stage(q, k, v, dil, state, last, tl=DL_SPAN, max_sub=4, ahead=4):
    B, _, L, D = q.shape
    first = state is None
    n_sub = min(max_sub, L // tl)
    TL = n_sub * tl
    cur = lambda b, r, i: (b, r, i, 0)
    prev = lambda b, r, i: (b, r, jnp.maximum(i * n_sub - 1, 0), 0)
    tile = pl.BlockSpec((1, 1, TL, D), cur)
    halo = pl.BlockSpec((1, 1, tl, D), prev)
    in_specs = [tile, halo, tile, halo, tile]
    args = [q, k, k, v, v]
    scratch = []
    if not first:
        fine = lambda b, r, i: (b, 0, r, i, 0)
        in_specs += [pl.BlockSpec((1, 4, 1, TL // 4, D), fine),
                     pl.BlockSpec((1, 4, 1, TL // 4, LANES), fine)]
        args += [state[0].reshape(B, 4, dil, L // 4, D), state[1].reshape(B, 4, dil, L // 4, LANES)]
        scratch = [pltpu.VMEM((D // LANES, TL, LANES), F32), pltpu.VMEM((TL, LANES), F32)]
    if last:
        out_shape = [jax.ShapeDtypeStruct((B, dil, L, D), BF16)]
        out_specs = [tile]
    else:
        out_shape = [jax.ShapeDtypeStruct((B, dil, L, D), F32),
                     jax.ShapeDtypeStruct((B, dil, L, LANES), F32)]
        out_specs = [tile, pl.BlockSpec((1, 1, TL, LANES), cur)]
    outs = pl.pallas_call(
        functools.partial(_dil_kernel, tl=tl, n_sub=n_sub, ahead=ahead, first=first, last=last),
        grid=(B, dil, L // TL),
        in_specs=in_specs,
        out_specs=out_specs,
        out_shape=out_shape,
        scratch_shapes=scratch,
        compiler_params=_params(("parallel", "parallel", "arbitrary")),
        name=f"dilated_attn_d{dil}",
    )(*args)
    return outs[0] if last else tuple(outs)


def _dilated_attention(qkv_by_dil):
    dils = sorted((d for _, d in DL_PAIRS), reverse=True)
    assert all(w // d == DL_SPAN for w, d in DL_PAIRS)
    assert all(a == 4 * b for a, b in zip(dils, dils[1:])) and dils[-1] == 1
    state = None
    for d in dils:
        state = _dilated_stage(*qkv_by_dil[d], d, state, last=(d == 1))
    return state


def _tail_kernel(*refs, n_mix, ffn_chunk):
    x_ref = refs[0]
    mix_refs = refs[1:1 + n_mix]
    wo_refs = refs[1 + n_mix:1 + 2 * n_mix]
    (g1_ref, b1_ref, w1_ref, w2_ref, g2_ref, b2_ref,
     p_ref, wp_ref, wg_ref, pg_ref, o_ref) = refs[1 + 2 * n_mix:]
    mix = jnp.dot(mix_refs[0][...], wo_refs[0][...], preferred_element_type=F32)
    for m_ref, w_ref in zip(mix_refs[1:], wo_refs[1:]):
        mix = mix + jnp.dot(m_ref[...], w_ref[...], preferred_element_type=F32)
    x1 = _layer_norm(ALPHA * x_ref[...] + mix, g1_ref[...], b1_ref[...])
    x1b = x1.astype(BF16)
    ffn = None
    for c in range(0, FFN_DIM, ffn_chunk):
        hdn = jnp.dot(x1b, w1_ref[:, c:c + ffn_chunk], preferred_element_type=F32)
        hdn = jnp.square(jnp.maximum(hdn, 0.0)).astype(BF16)
        part = jnp.dot(hdn, w2_ref[c:c + ffn_chunk, :], preferred_element_type=F32)
        ffn = part if ffn is None else ffn + part
    x2 = _layer_norm(ALPHA * x1 + ffn, g2_ref[...], b2_ref[...])
    e = _rms_norm(jnp.dot(p_ref[...].astype(BF16), wp_ref[...], preferred_element_type=F32),
                  pg_ref[...])
    gate = _sigmoid(jnp.dot(x2.astype(BF16), wg_ref[...], preferred_element_type=F32))
    o_ref[...] = x2 + gate * e


def _layer_tail(x2, mixes, w_outs, g1, b1, w1, w2, g2, b2, p_all, layer, wp, wg, pg, tm=512,
                ffn_chunk=1024):
    T, D = x2.shape
    rows = lambda w: pl.BlockSpec((tm, w), lambda i: (i, 0))
    vec = lambda a: a.reshape(1, -1)
    in_specs = ([rows(D)] + [rows(m.shape[1]) for m in mixes]
                + [_resident(w.shape) for w in w_outs]
                + [_resident((1, D)), _resident((1, D)), _resident(w1.shape),
                   _resident(w2.shape), _resident((1, D)), _resident((1, D)),
                   pl.BlockSpec((None, tm, PLE_DIM), lambda i: (layer, i, 0)),
                   _resident(wp.shape), _resident(wg.shape), _resident((1, D))])
    return pl.pallas_call(
        functools.partial(_tail_kernel, n_mix=len(mixes), ffn_chunk=ffn_chunk),
        grid=(T // tm,),
        in_specs=in_specs,
        out_specs=rows(D),
        out_shape=jax.ShapeDtypeStruct((T, D), F32),
        compiler_params=_params(("parallel",)),
        name="layer_tail",
    )(x2, *mixes, *w_outs, vec(g1), vec(b1), w1, w2, vec(g2), vec(b2), p_all, wp, wg, vec(pg))


def kernel(x, p, ev_w_in, ev_w_out, da_lambda, da_subln_g, hg_lb_logits, hg_norm_g,
           od_w_in, od_w_out, ln1_g, ln1_b, ffn_w1, ffn_w2, ln2_g, ln2_b,
           ple_w_proj, ple_w_gate, ple_norm_g):
    B, S, D = x.shape
    T = B * S
    tables = _rope_lane_tables(S)
    lb_all = jnp.cumsum(jax.nn.softmax(hg_lb_logits.astype(F32), axis=0), axis=0)
    bf = lambda w: w.astype(BF16)
    x2 = x.reshape(T, D)
    for l in range(DEPTH):
        j = l // 2
        if l % 2 == 0:
            lam_init = 0.8 - 0.6 * math.exp(-0.3 * l)
            lp = da_lambda[j].astype(F32)
            lam = (jnp.exp(jnp.sum(lp[0] * lp[1])) - jnp.exp(jnp.sum(lp[2] * lp[3]))
                   + lam_init).reshape(1)
            q, k, vt, o_b = _project_layer0(x2, bf(ev_w_in[j]), tables, lb_all[j].reshape(1, -1),
                                            hg_norm_g[j].reshape(1, -1), S)
            o_a = _diff_attention(q.reshape(B, S, -1), k.reshape(B, S, -1), vt,
                                  lam, da_subln_g[j].reshape(1, -1), lam_init)
            mixes = [o_a.reshape(T, -1), o_b.reshape(T, -1)]
            w_o = bf(ev_w_out[j])
            w_outs = [w_o[:DA_WIDTH], w_o[DA_WIDTH:]]
        else:
            segs = ((0, D, "q"), (D, D, "k"), (2 * D, D, "plain"))
            coarse = tuple(sorted(d for _, d in DL_PAIRS if d > 1))
            outs = _project(x2, bf(od_w_in[j]), tables, S, segs, coarse)
            per = 1 + len(coarse)
            qkv = {1: tuple(outs[n * per].reshape(B, 1, S, D) for n in range(3))}
            for m, d in enumerate(coarse):
                qkv[d] = tuple(outs[n * per + 1 + m] for n in range(3))
            o = _dilated_attention(qkv)
            mixes = [o.reshape(T, D)]
            w_outs = [bf(od_w_out[j])]
        x2 = _layer_tail(x2, mixes, w_outs, ln1_g[l], ln1_b[l], bf(ffn_w1[l]), bf(ffn_w2[l]),
                         ln2_g[l], ln2_b[l], p.reshape(DEPTH, T, PLE_DIM), l, bf(ple_w_proj[l]),
                         bf(ple_w_gate[l]), ple_norm_g[l])
    return x2.reshape(B, S, D)
```

```python
import functools
import math

import jax
import jax.numpy as jnp
import numpy as np
from jax import lax
from jax.experimental import pallas as pl
from jax.experimental.pallas import tpu as pltpu

D_MODEL = 1024
DEPTH = 2
PLE_DIM = 256

DA_HEADS = 4
DA_HEAD_DIM = 64
DA_V_DIM = 2 * DA_HEAD_DIM
DA_QK_WIDTH = DA_HEADS * 2 * DA_HEAD_DIM
DA_WIDTH = DA_HEADS * DA_V_DIM
DA_TQ = 512
DA_VT_ROWS = DA_V_DIM + 16
LOG2_E = math.log2(math.e)

HG_HEADS = 4
HG_KDIM = 128
HG_VDIM = 128
HG_KWIDTH = HG_HEADS * HG_KDIM
HG_WIDTH = HG_HEADS * HG_VDIM
HG_CHUNK = 64

DL_HEADS = 16
DL_HEAD_DIM = D_MODEL // DL_HEADS
DL_PAIRS = ((128, 1), (512, 4), (2048, 16))
DL_SPAN = 128

FFN_DIM = 4 * D_MODEL
ROPE_THETA = 500000.0
ROT_DIM = 16
ALPHA = (2 * DEPTH) ** 0.25
LN_EPS = 1e-5

LANES = 128
VMEM_LIMIT_BYTES = 56 * 1024 * 1024

F32 = jnp.float32
BF16 = jnp.bfloat16
NEG_INF = float("-inf")


def _params(sem, vmem=VMEM_LIMIT_BYTES):
    return pltpu.CompilerParams(dimension_semantics=sem, vmem_limit_bytes=vmem)


def _resident(shape):
    nd = len(shape)
    return pl.BlockSpec(shape, lambda *_: (0,) * nd, pipeline_mode=pl.Buffered(1))


def _sigmoid(x):
    return 1.0 / (1.0 + jnp.exp(-x))


def _layer_norm(x, g, b):
    mu = jnp.mean(x, axis=-1, keepdims=True)
    xc = x - mu
    var = jnp.mean(xc * xc, axis=-1, keepdims=True)
    return xc * lax.rsqrt(var + LN_EPS) * g + b


def _rms_norm(x, g):
    return x * lax.rsqrt(jnp.mean(x * x, axis=-1, keepdims=True) + LN_EPS) * g


def _rope_lane_tables(seq):
    half = ROT_DIM // 2
    f32 = np.float32
    inv = f32(ROPE_THETA) ** (-np.arange(0, ROT_DIM, 2, dtype=f32) / f32(ROT_DIM))
    lane = np.arange(LANES) % 64
    inv_lane = np.where(lane < ROT_DIM, inv[lane % half], f32(0.0)).astype(f32)
    ang = np.arange(seq, dtype=f32)[:, None] * inv_lane[None, :]
    cos, sin = np.cos(ang), np.sin(ang)
    s1 = np.where((lane >= half) & (lane < ROT_DIM), sin, f32(0.0))
    s2 = np.where(lane < half, -sin, f32(0.0))
    return tuple(jnp.asarray(t, dtype=F32) for t in (cos, s1, s2))


def _rope(acc, c, s1, s2, scale):
    parts = []
    for g in range(acc.shape[1] // LANES):
        a = acc[:, g * LANES:(g + 1) * LANES]
        r = (a * c + pltpu.roll(a, ROT_DIM // 2, 1) * s1
             + pltpu.roll(a, LANES - ROT_DIM // 2, 1) * s2)
        parts.append(r if scale is None else r * scale)
    return jnp.concatenate(parts, axis=1)


Q_SCALE = 64 ** -0.5 * LOG2_E


def _proj_kernel(x_ref, w_ref, c_ref, s1_ref, s2_ref, *refs, segs, dils, col_chunk, tm):
    n_out = len(segs) * (1 + len(dils))
    out_refs, stage = refs[:n_out], refs[n_out:]
    xb = x_ref[...].astype(BF16)
    c, s1, s2 = c_ref[...], s1_ref[...], s2_ref[...]
    for n, (start, width, kind) in enumerate(segs):
        nat_ref = out_refs[n * (1 + len(dils))]
        dil_refs = out_refs[n * (1 + len(dils)) + 1:(n + 1) * (1 + len(dils))]
        for j in range(0, width, col_chunk):
            acc = jnp.dot(xb, w_ref[:, start + j:start + j + col_chunk],
                          preferred_element_type=F32)
            if kind != "plain":
                acc = _rope(acc, c, s1, s2, {"q": Q_SCALE, "k": None}[kind])
            nat_ref[:, j:j + col_chunk] = acc.astype(nat_ref.dtype)
            for g in range(col_chunk // LANES):
                cols = slice(j + g * LANES, j + (g + 1) * LANES)
                src, d_prev = stage[0], 1
                src[g] = acc[:, g * LANES:(g + 1) * LANES]
                for n_d, (d, ref) in enumerate(zip(dils, dil_refs)):
                    ratio, rows = d // d_prev, tm // d
                    dst = stage[(n_d + 1) % 2] if n_d + 1 < len(dils) else None
                    for r_prev in range(d_prev):
                        for c_new in range(ratio):
                            r = r_prev + d_prev * c_new
                            piece = src[g, pl.ds(r_prev * (tm // d_prev) + c_new, rows,
                                                 stride=ratio), :]
                            ref[0, r, :, cols] = piece.astype(ref.dtype)
                            if dst is not None:
                                dst[g, r * rows:(r + 1) * rows, :] = piece
                    src, d_prev = dst, d


def _project(x2, w_bf, tables, seq, segs, dils, tm=512, col_chunk=512):
    T, D = x2.shape
    N = w_bf.shape[1]
    nsb = seq // tm
    assert dils and all(b % a == 0 for a, b in zip((1,) + tuple(dils), dils))
    tab_spec = pl.BlockSpec((tm, LANES), lambda i: (i % nsb, 0))
    out_shape, out_specs = [], []
    for (_, w, _) in segs:
        out_shape.append(jax.ShapeDtypeStruct((T, w), BF16))
        out_specs.append(pl.BlockSpec((tm, w), lambda i: (i, 0)))
        for d in dils:
            out_shape.append(jax.ShapeDtypeStruct((T // seq, d, seq // d, w), BF16))
            out_specs.append(pl.BlockSpec((1, d, tm // d, w), lambda i: (i // nsb, 0, i % nsb, 0)))
    return pl.pallas_call(
        functools.partial(_proj_kernel, segs=segs, dils=dils, col_chunk=col_chunk, tm=tm),
        grid=(T // tm,),
        in_specs=[pl.BlockSpec((tm, D), lambda i: (i, 0)), _resident((D, N)),
                  tab_spec, tab_spec, tab_spec],
        out_specs=out_specs,
        out_shape=out_shape,
        scratch_shapes=[pltpu.VMEM((col_chunk // LANES, tm, LANES), F32)] * min(len(dils), 2),
        compiler_params=_params(("parallel",)),
        name="in_proj",
    )(x2, w_bf, *tables)


def _diffattn_kernel(lam_ref, q_ref, k_ref, vt_ref, g_ref, o_ref, acc_sc,
                     *, tq, tk, unroll, ahead, lam_init):
    qi = pl.program_id(2)
    n_diag = tq // tk
    q = q_ref[0]
    lane = lax.broadcasted_iota(jnp.int32, (tq, LANES), 1)
    zero = jnp.zeros_like(q)
    qs = jnp.concatenate([jnp.where(lane < DA_HEAD_DIM, q, zero),
                          jnp.where(lane >= DA_HEAD_DIM, q, zero)], axis=0)
    acc_sc[...] = jnp.zeros(acc_sc.shape, F32)

    def scores(blk, n_map):
        start = pl.multiple_of(blk * tk, tk)
        return lax.dot_general(k_ref[0, pl.ds(start, tk), :], qs[n_map * tq:(n_map + 1) * tq],
                               (((1,), (1,)), ((), ())),
                               preferred_element_type=F32)

    def update(s, blk, n_map, m_prev, diag):
        cols = slice(n_map * tq, (n_map + 1) * tq)
        if diag is not None:
            krow = diag * tk + lax.broadcasted_iota(jnp.int32, (tk, tq), 0)
            col = lax.broadcasted_iota(jnp.int32, (tk, tq), 1)
            s = jnp.where(krow <= col, s, NEG_INF)
        m_new = jnp.maximum(m_prev, jnp.max(s, axis=0, keepdims=True))
        alpha = jnp.exp2(m_prev - m_new)
        p = jnp.exp2(s - m_new).astype(BF16)
        pv = jnp.dot(vt_ref[0, 0, blk], p, preferred_element_type=F32)
        acc_sc[:, cols] = alpha * acc_sc[:, cols] + pv
        return m_new

    def run(first_blk, n_plain, m, with_diag):
        n = n_plain + (n_diag if with_diag else 0)
        items = [(u, n_map) for u in range(n) for n_map in range(2)]
        m = list(m)
        pending = [scores(first_blk + u, n_map) for u, n_map in items[:ahead]]
        for idx, (u, n_map) in enumerate(items):
            if idx + ahead < len(items):
                nu, nm = items[idx + ahead]
                pending.append(scores(first_blk + nu, nm))
            m[n_map] = update(pending[idx], first_blk + u, n_map, m[n_map],
                              u - n_plain if u >= n_plain else None)
            pending[idx] = None
        return tuple(m)

    n_before = qi * n_diag
    n_groups = n_before // unroll
    m = (jnp.full((1, tq), NEG_INF, F32),) * 2
    m = lax.fori_loop(0, n_groups, lambda j, c: run(j * unroll, unroll, c, False), m)
    step = math.gcd(n_diag, unroll)
    tails = [functools.partial(run, n_groups * unroll, r, with_diag=True)
             for r in range(0, unroll, step)]
    lax.switch((n_before - n_groups * unroll) // step, tails, m)

    acc = acc_sc[0:DA_V_DIM, :] / acc_sc[DA_V_DIM:DA_V_DIM + 1, :]
    o = acc[:, :tq] - lam_ref[0] * acc[:, tq:]
    o = o * lax.rsqrt(jnp.mean(o * o, axis=0, keepdims=True) + LN_EPS)
    o_ref[0] = (o.T * g_ref[...] * (1.0 - lam_init)).astype(o_ref.dtype)


def _diff_attention(q, k, vt, lam, sub_g, lam_init, tq=512, unroll=8, ahead=4):
    B, S, _ = q.shape
    tk = DA_TQ
    return pl.pallas_call(
        functools.partial(_diffattn_kernel, tq=tq, tk=tk, unroll=unroll, ahead=ahead,
                          lam_init=lam_init),
        grid=(B, DA_HEADS, S // tq),
        in_specs=[
            pl.BlockSpec(memory_space=pltpu.SMEM),
            pl.BlockSpec((1, tq, LANES), lambda b, h, i: (b, i, h)),
            pl.BlockSpec((1, S, LANES), lambda b, h, i: (b, 0, h)),
            pl.BlockSpec((1, 1, S // tk, DA_VT_ROWS, tk), lambda b, h, i: (b, h, 0, 0, 0)),
            pl.BlockSpec((1, DA_V_DIM), lambda b, h, i: (0, 0)),
        ],
        out_specs=pl.BlockSpec((1, tq, LANES), lambda b, h, i: (b, i, h)),
        out_shape=jax.ShapeDtypeStruct((B, S, DA_WIDTH), BF16),
        scratch_shapes=[pltpu.VMEM((DA_VT_ROWS, 2 * tq), F32)],
        compiler_params=_params(("parallel", "parallel", "arbitrary")),
        name="diff_attn",
    )(lam, q, k, vt, sub_g)


class _HgrnTile:
    def __init__(self, hq, hf, hi, hg, lb, og):
        self.hq, self.hf, self.hg, self.lb, self.og = hq, hf, hg, lb, og
        self.vb = hi.astype(BF16)
        self.tc = hq.shape[0]
        self.n = self.tc // HG_CHUNK

    def gates(self):
        self.q = self.hq * _sigmoid(self.hq)
        f = self.lb + (1.0 - self.lb) * _sigmoid(self.hf)
        self.kk = 1.0 - f
        self.logf = jnp.log(f)

    def decays(self):
        C, n, tc, kw = HG_CHUNK, self.n, self.tc, HG_KWIDTH
        ri = lax.broadcasted_iota(jnp.int32, (tc, tc), 0)
        ci = lax.broadcasted_iota(jnp.int32, (tc, tc), 1)
        self.tril = (ci <= ri) & (ci >= ri - ri % C)
        hi = self.logf.astype(BF16)
        rest = self.logf - hi.astype(F32)
        mid = rest.astype(BF16)
        lo = (rest - mid.astype(F32)).astype(BF16)
        parts = jnp.dot(self.tril.astype(BF16), jnp.concatenate([hi, mid, lo], axis=1),
                        preferred_element_type=F32)
        b = (parts[:, :kw] + parts[:, kw:2 * kw] + parts[:, 2 * kw:]).reshape(n, C, kw)
        b_last = b[:, C - 1:C, :]
        b_mid = b[:, C // 2 - 1:C // 2, :]
        q3 = self.q.reshape(n, C, kw)
        k3 = self.kk.reshape(n, C, kw)
        flat = lambda t: t.reshape(tc, kw).astype(BF16)
        self.qa = flat(q3 * jnp.exp(b - b_mid))
        self.ka = flat(k3 * jnp.exp(b_mid - b))
        self.q_out = flat(q3 * jnp.exp(b))
        self.k_st = flat(k3 * jnp.exp(b_last - b))
        self.decay = jnp.exp(b_last)

    def intra(self):
        self.o = []
        for h in range(HG_HEADS):
            kc = slice(h * HG_KDIM, (h + 1) * HG_KDIM)
            vc = slice(h * HG_VDIM, (h + 1) * HG_VDIM)
            a = lax.dot_general(self.qa[:, kc], self.ka[:, kc], (((1,), (1,)), ((), ())),
                                preferred_element_type=F32)
            a = jnp.where(self.tril, a, 0.0).astype(BF16)
            self.o.append(jnp.dot(a, self.vb[:, vc], preferred_element_type=F32))

    def recur(self, states):
        C = HG_CHUNK
        chunk = lambda c: slice(c * C, (c + 1) * C)
        kcol = lambda h: slice(h * HG_KDIM, (h + 1) * HG_KDIM)
        vcol = lambda h: slice(h * HG_VDIM, (h + 1) * HG_VDIM)
        upd = [[lax.dot_general(self.vb[chunk(c), vcol(h)], self.k_st[chunk(c), kcol(h)],
                                (((0,), (0,)), ((), ())), preferred_element_type=F32)
                for c in range(self.n)] for h in range(HG_HEADS)]
        before = [[None] * self.n for _ in range(HG_HEADS)]
        for h in range(HG_HEADS):
            st = states[h]
            for c in range(self.n):
                before[h][c] = st.astype(BF16)
                st = self.decay[c, :, kcol(h)] * st + upd[h][c]
            states = states[:h] + [st] + states[h + 1:]
        for h in range(HG_HEADS):
            inter = [lax.dot_general(self.q_out[chunk(c), kcol(h)], before[h][c],
                                     (((1,), (1,)), ((), ())), preferred_element_type=F32)
                     for c in range(self.n)]
            self.o[h] = self.o[h] + jnp.concatenate(inter, axis=0)
        return states

    def finish(self, o_ref, rows):
        for h in range(HG_HEADS):
            vc = slice(h * HG_VDIM, (h + 1) * HG_VDIM)
            hg = self.hg[:, vc]
            o = _rms_norm(self.o[h], self.og) * (hg * _sigmoid(hg))
            o_ref[rows, vc] = o.astype(o_ref.dtype)


def _proj0_kernel(x_ref, w_ref, c_ref, s1_ref, s2_ref, lb_ref, og_ref,
                  q_ref, k_ref, vt_ref, ob_ref, state_sc, *, tm, tc, nsb):
    @pl.when(pl.program_id(0) % nsb == 0)
    def _():
        state_sc[...] = jnp.zeros(state_sc.shape, F32)

    xb = x_ref[...].astype(BF16)
    proj = lambda start, width: jnp.dot(xb, w_ref[:, start:start + width],
                                        preferred_element_type=F32)
    g0 = 2 * DA_QK_WIDTH + DA_WIDTH
    hq = proj(g0, HG_KWIDTH)
    hf = proj(g0 + HG_KWIDTH, HG_KWIDTH)
    hi = proj(g0 + 2 * HG_KWIDTH, HG_WIDTH)
    hg = proj(g0 + 2 * HG_KWIDTH + HG_WIDTH, HG_WIDTH)
    lb, og = lb_ref[...], og_ref[...]
    tiles = [_HgrnTile(hq[r:r + tc], hf[r:r + tc], hi[r:r + tc], hg[r:r + tc], lb, og)
             for r in range(0, tm, tc)]

    cw = 2 * LANES

    def emit_qk(out_ref, start, j, scale):
        acc = _rope(proj(start + j, cw), c_ref[...], s1_ref[...], s2_ref[...], scale)
        out_ref[:, j:j + cw] = acc.astype(out_ref.dtype)

    def emit_vt(j):
        acc = proj(2 * DA_QK_WIDTH + j, cw)
        for g in range(cw // DA_V_DIM):
            head = j // DA_V_DIM + g
            for sb in range(tm // DA_TQ):
                t = acc[sb * DA_TQ:(sb + 1) * DA_TQ, g * DA_V_DIM:(g + 1) * DA_V_DIM].T
                vt_ref[0, head, sb, 0:DA_V_DIM, :] = t.astype(vt_ref.dtype)
                vt_ref[0, head, sb, DA_V_DIM:, :] = jnp.ones(
                    (DA_VT_ROWS - DA_V_DIM, DA_TQ), vt_ref.dtype)

    big = ([functools.partial(emit_qk, q_ref, 0, j, Q_SCALE) for j in range(0, DA_QK_WIDTH, cw)]
           + [functools.partial(emit_qk, k_ref, DA_QK_WIDTH, j, None)
              for j in range(0, DA_QK_WIDTH, cw)]
           + [functools.partial(emit_vt, j) for j in range(0, DA_WIDTH, cw)])
    states = [state_sc[h] for h in range(HG_HEADS)]

    def interleaved(phase):
        if big:
            big.pop(0)()
        return phase()

    for t in tiles:
        t.gates()
    for t in tiles:
        interleaved(t.decays)
    for t in tiles:
        interleaved(t.intra)
    for t in tiles:
        states = interleaved(functools.partial(t.recur, states))
    for n, t in enumerate(tiles):
        interleaved(functools.partial(t.finish, ob_ref, slice(n * tc, (n + 1) * tc)))
    while big:
        big.pop(0)()
    for h in range(HG_HEADS):
        state_sc[h] = states[h]


def _project_layer0(x2, w_bf, tables, lb, out_g, seq, tm=512, tc=256):
    T, D = x2.shape
    N = w_bf.shape[1]
    nsb = seq // tm
    rows = lambda w: pl.BlockSpec((tm, w), lambda i: (i, 0))
    tab_spec = pl.BlockSpec((tm, LANES), lambda i: (i % nsb, 0))
    vt_shape = (T // seq, DA_HEADS, seq // DA_TQ, DA_VT_ROWS, DA_TQ)
    return pl.pallas_call(
        functools.partial(_proj0_kernel, tm=tm, tc=tc, nsb=nsb),
        grid=(T // tm,),
        in_specs=[rows(D), _resident((D, N)), tab_spec, tab_spec, tab_spec,
                  _resident((1, HG_KWIDTH)), _resident((1, HG_VDIM))],
        out_specs=[rows(DA_QK_WIDTH), rows(DA_QK_WIDTH),
                   pl.BlockSpec((1, DA_HEADS, tm // DA_TQ, DA_VT_ROWS, DA_TQ),
                                lambda i: (i // nsb, 0, i % nsb, 0, 0)),
                   rows(HG_WIDTH)],
        out_shape=[jax.ShapeDtypeStruct((T, DA_QK_WIDTH), BF16),
                   jax.ShapeDtypeStruct((T, DA_QK_WIDTH), BF16),
                   jax.ShapeDtypeStruct(vt_shape, BF16),
                   jax.ShapeDtypeStruct((T, HG_WIDTH), BF16)],
        scratch_shapes=[pltpu.VMEM((HG_HEADS, HG_VDIM, HG_KDIM), F32)],
        compiler_params=_params(("arbitrary",)),
        name="in_proj_hgrn2",
    )(x2, w_bf, *tables, lb, out_g)


def _dil_kernel(*refs, tl, n_sub, ahead, first, last):
    q_ref, kp_ref, kc_ref, vp_ref, vc_ref = refs[:5]
    refs = refs[5:]
    if not first:
        accp_ref, mlp_ref = refs[:2]
        refs = refs[2:]
    if last:
        o_ref = refs[0]
        refs = refs[1:]
    else:
        acc_ref, ml_ref = refs[:2]
        refs = refs[2:]
    i = pl.program_id(2)
    n_pairs = DL_HEADS // 2
    half = DL_HEAD_DIM

    if not first:
        acc_sc, ml_sc = refs
        quarter = n_sub * tl // 4
        for c in range(4):
            for hp in range(n_pairs):
                acc_sc[hp, pl.ds(c, quarter, stride=4), :] = (
                    accp_ref[0, c, 0, :, hp * LANES:(hp + 1) * LANES])
            ml_sc[pl.ds(c, quarter, stride=4), :] = mlp_ref[0, c, 0]

    lane = lax.broadcasted_iota(jnp.int32, (tl, LANES), 1)
    low = lane < half

    def pair_rows(x):
        return jnp.concatenate([jnp.broadcast_to(x[:, :tl], (half, tl)),
                                jnp.broadcast_to(x[:, tl:], (half, tl))], axis=0)

    def window(prev_ref, cur_ref, n, cols):
        if n == 0:
            return jnp.concatenate([prev_ref[0, 0, :, cols], cur_ref[0, 0, 0:tl, cols]], axis=0)
        return cur_ref[0, 0, (n - 1) * tl:(n + 1) * tl, cols]

    def rows_of(n):
        return slice(n * tl, (n + 1) * tl)

    def scores(n, hp):
        cols = slice(hp * LANES, (hp + 1) * LANES)
        qp = q_ref[0, 0, rows_of(n), cols]
        zero = jnp.zeros_like(qp)
        qs = jnp.concatenate([jnp.where(low, qp, zero), jnp.where(low, zero, qp)], axis=0)
        return lax.dot_general(window(kp_ref, kc_ref, n, cols), qs, (((1,), (1,)), ((), ())),
                               preferred_element_type=F32)

    def update(s, n, hp, ml_t):
        cols = slice(hp * LANES, (hp + 1) * LANES)
        key = lax.broadcasted_iota(jnp.int32, (2 * tl, 2 * tl), 0)
        qcol = lax.broadcasted_iota(jnp.int32, (2 * tl, 2 * tl), 1)
        dist = tl + jnp.where(qcol >= tl, qcol - tl, qcol) - key
        valid = (dist >= 0) & (dist <= DL_SPAN)
        if n == 0:
            valid = valid & ((key >= tl) | (i > 0))
        s = jnp.where(valid, s, NEG_INF)
        m_new = jnp.max(s, axis=0, keepdims=True)
        if not first:
            h0 = 2 * hp
            m_prev = jnp.concatenate([ml_t[h0:h0 + 1], ml_t[h0 + 1:h0 + 2]], axis=1)
            l_prev = jnp.concatenate([ml_t[DL_HEADS + h0:DL_HEADS + h0 + 1],
                                      ml_t[DL_HEADS + h0 + 1:DL_HEADS + h0 + 2]], axis=1)
            m_new = jnp.maximum(m_prev, m_new)
            alpha = jnp.exp2(m_prev - m_new)
        p = jnp.exp2(s - m_new)
        l_new = jnp.sum(p, axis=0, keepdims=True)
        pv = lax.dot_general(window(vp_ref, vc_ref, n, cols), p.astype(BF16),
                             (((0,), (0,)), ((), ())), preferred_element_type=F32)
        o_t = jnp.concatenate([pv[:half, :tl], pv[half:, tl:]], axis=0)
        if not first:
            l_new = l_new + alpha * l_prev
            o_t = o_t + pair_rows(alpha) * acc_sc[hp, rows_of(n), :].T
        if last:
            o_ref[0, 0, rows_of(n), cols] = (o_t * pair_rows(1.0 / l_new)).T.astype(o_ref.dtype)
        else:
            acc_ref[0, 0, rows_of(n), cols] = o_t.T
        return m_new, l_new

    blocks = [(n, hp) for n in range(n_sub) for hp in range(n_pairs)]
    pending = [scores(*blk) for blk in blocks[:ahead]]
    for idx, (n, hp) in enumerate(blocks):
        if idx + ahead < len(blocks):
            pending.append(scores(*blocks[idx + ahead]))
        if hp == 0:
            ml_t = None if first else ml_sc[rows_of(n), :].T
            ms, ls = [], []
        m_new, l_new = update(pending[idx], n, hp, ml_t)
        pending[idx] = None
        ms += [m_new[:, :tl], m_new[:, tl:]]
        ls += [l_new[:, :tl], l_new[:, tl:]]
        if hp == n_pairs - 1 and not last:
            pad = jnp.zeros((LANES - 2 * DL_HEADS, tl), F32)
            ml_ref[0, 0, rows_of(n), :] = jnp.concatenate(ms + ls + [pad], axis=0).T


def _dilated_stage(q, k, v, dil, state, last, tl=DL_SPAN, max_sub=4, ahead=4):
    B, _, L, D = q.shape
    first = state is None
    n_sub = min(max_sub, L // tl)
    TL = n_sub * tl
    cur = lambda b, r, i: (b, r, i, 0)
    prev = lambda b, r, i: (b, r, jnp.maximum(i * n_sub - 1, 0), 0)
    tile = pl.BlockSpec((1, 1, TL, D), cur)
    halo = pl.BlockSpec((1, 1, tl, D), prev)
    in_specs = [tile, halo, tile, halo, tile]
    args = [q, k, k, v, v]
    scratch = []
    if not first:
        fine = lambda b, r, i: (b, 0, r, i, 0)
        in_specs += [pl.BlockSpec((1, 4, 1, TL // 4, D), fine),
                     pl.BlockSpec((1, 4, 1, TL // 4, LANES), fine)]
        args += [state[0].reshape(B, 4, dil, L // 4, D), state[1].reshape(B, 4, dil, L // 4, LANES)]
        scratch = [pltpu.VMEM((D // LANES, TL, LANES), F32), pltpu.VMEM((TL, LANES), F32)]
    if last:
        out_shape = [jax.ShapeDtypeStruct((B, dil, L, D), BF16)]
        out_specs = [tile]
    else:
        out_shape = [jax.ShapeDtypeStruct((B, dil, L, D), F32),
                     jax.ShapeDtypeStruct((B, dil, L, LANES), F32)]
        out_specs = [tile, pl.BlockSpec((1, 1, TL, LANES), cur)]
    outs = pl.pallas_call(
        functools.partial(_dil_kernel, tl=tl, n_sub=n_sub, ahead=ahead, first=first, last=last),
        grid=(B, dil, L // TL),
        in_specs=in_specs,
        out_specs=out_specs,
        out_shape=out_shape,
        scratch_shapes=scratch,
        compiler_params=_params(("parallel", "parallel", "arbitrary")),
        name=f"dilated_attn_d{dil}",
    )(*args)
    return outs[0] if last else tuple(outs)


def _dilated_attention(qkv_by_dil):
    dils = sorted((d for _, d in DL_PAIRS), reverse=True)
    assert all(w // d == DL_SPAN for w, d in DL_PAIRS)
    assert all(a == 4 * b for a, b in zip(dils, dils[1:])) and dils[-1] == 1
    state = None
    for d in dils:
        state = _dilated_stage(*qkv_by_dil[d], d, state, last=(d == 1))
    return state


def _tail_kernel(*refs, n_mix, ffn_chunk):
    x_ref = refs[0]
    mix_refs = refs[1:1 + n_mix]
    wo_refs = refs[1 + n_mix:1 + 2 * n_mix]
    (g1_ref, b1_ref, w1_ref, w2_ref, g2_ref, b2_ref,
     p_ref, wp_ref, wg_ref, pg_ref, o_ref) = refs[1 + 2 * n_mix:]
    mix = jnp.dot(mix_refs[0][...], wo_refs[0][...], preferred_element_type=F32)
    for m_ref, w_ref in zip(mix_refs[1:], wo_refs[1:]):
        mix = mix + jnp.dot(m_ref[...], w_ref[...], preferred_element_type=F32)
    x1 = _layer_norm(ALPHA * x_ref[...] + mix, g1_ref[...], b1_ref[...])
    x1b = x1.astype(BF16)
    ffn = None
    for c in range(0, FFN_DIM, ffn_chunk):
        hdn = jnp.dot(x1b, w1_ref[:, c:c + ffn_chunk], preferred_element_type=F32)
        hdn = jnp.square(jnp.maximum(hdn, 0.0)).astype(BF16)
        part = jnp.dot(hdn, w2_ref[c:c + ffn_chunk, :], preferred_element_type=F32)
        ffn = part if ffn is None else ffn + part
    x2 = _layer_norm(ALPHA * x1 + ffn, g2_ref[...], b2_ref[...])
    e = _rms_norm(jnp.dot(p_ref[...].astype(BF16), wp_ref[...], preferred_element_type=F32),
                  pg_ref[...])
    gate = _sigmoid(jnp.dot(x2.astype(BF16), wg_ref[...], preferred_element_type=F32))
    o_ref[...] = x2 + gate * e


def _layer_tail(x2, mixes, w_outs, g1, b1, w1, w2, g2, b2, p_all, layer, wp, wg, pg, tm=512,
                ffn_chunk=1024):
    T, D = x2.shape
    rows = lambda w: pl.BlockSpec((tm, w), lambda i: (i, 0))
    vec = lambda a: a.reshape(1, -1)
    in_specs = ([rows(D)] + [rows(m.shape[1]) for m in mixes]
                + [_resident(w.shape) for w in w_outs]
                + [_resident((1, D)), _resident((1, D)), _resident(w1.shape),
                   _resident(w2.shape), _resident((1, D)), _resident((1, D)),
                   pl.BlockSpec((None, tm, PLE_DIM), lambda i: (layer, i, 0)),
                   _resident(wp.shape), _resident(wg.shape), _resident((1, D))])
    return pl.pallas_call(
        functools.partial(_tail_kernel, n_mix=len(mixes), ffn_chunk=ffn_chunk),
        grid=(T // tm,),
        in_specs=in_specs,
        out_specs=rows(D),
        out_shape=jax.ShapeDtypeStruct((T, D), F32),
        compiler_params=_params(("parallel",)),
        name="layer_tail",
    )(x2, *mixes, *w_outs, vec(g1), vec(b1), w1, w2, vec(g2), vec(b2), p_all, wp, wg, vec(pg))


def kernel(x, p, ev_w_in, ev_w_out, da_lambda, da_subln_g, hg_lb_logits, hg_norm_g,
           od_w_in, od_w_out, ln1_g, ln1_b, ffn_w1, ffn_w2, ln2_g, ln2_b,
           ple_w_proj, ple_w_gate, ple_norm_g):
    B, S, D = x.shape
    T = B * S
    tables = _rope_lane_tables(S)
    lb_all = jnp.cumsum(jax.nn.softmax(hg_lb_logits.astype(F32), axis=0), axis=0)
    bf = lambda w: w.astype(BF16)
    x2 = x.reshape(T, D)
    for l in range(DEPTH):
        j = l // 2
        if l % 2 == 0:
            lam_init = 0.8 - 0.6 * math.exp(-0.3 * l)
            lp = da_lambda[j].astype(F32)
            lam = (jnp.exp(jnp.sum(lp[0] * lp[1])) - jnp.exp(jnp.sum(lp[2] * lp[3]))
                   + lam_init).reshape(1)
            q, k, vt, o_b = _project_layer0(x2, bf(ev_w_in[j]), tables, lb_all[j].reshape(1, -1),
                                            hg_norm_g[j].reshape(1, -1), S)
            o_a = _diff_attention(q.reshape(B, S, -1), k.reshape(B, S, -1), vt,
                                  lam, da_subln_g[j].reshape(1, -1), lam_init)
            mixes = [o_a.reshape(T, -1), o_b.reshape(T, -1)]
            w_o = bf(ev_w_out[j])
            w_outs = [w_o[:DA_WIDTH], w_o[DA_WIDTH:]]
        else:
            segs = ((0, D, "q"), (D, D, "k"), (2 * D, D, "plain"))
            coarse = tuple(sorted(d for _, d in DL_PAIRS if d > 1))
            outs = _project(x2, bf(od_w_in[j]), tables, S, segs, coarse)
            per = 1 + len(coarse)
            qkv = {1: tuple(outs[n * per].reshape(B, 1, S, D) for n in range(3))}
            for m, d in enumerate(coarse):
                qkv[d] = tuple(outs[n * per + 1 + m] for n in range(3))
            o = _dilated_attention(qkv)
            mixes = [o.reshape(T, D)]
            w_outs = [bf(od_w_out[j])]
        x2 = _layer_tail(x2, mixes, w_outs, ln1_g[l], ln1_b[l], bf(ffn_w1[l]), bf(ffn_w2[l]),
                         ln2_g[l], ln2_b[l], p.reshape(DEPTH, T, PLE_DIM), l, bf(ple_w_proj[l]),
                         bf(ple_w_gate[l]), ple_norm_g[l])
    return x2.reshape(B, S, D)
```
